```python
import math
import jax, jax.numpy as jnp
from jax import lax
import numpy as np

D_MODEL = 1024
BATCH = 16
SEQ = 4096
DEPTH = 4

HEAD_DIM = 64
MIX_WIDTH = D_MODEL
DIFF_WIDTH = MIX_WIDTH // 2
DIFF_HEADS = DIFF_WIDTH // (2 * HEAD_DIM)
SWA_WIDTH = MIX_WIDTH - DIFF_WIDTH
SWA_Q_HEADS = SWA_WIDTH // HEAD_DIM
SWA_KV_HEADS = 2
SWA_GROUP = SWA_Q_HEADS // SWA_KV_HEADS
WINDOW = 128
BLOCK = 128
D_FF = ((8 * D_MODEL // 3 + 127) // 128) * 128
ROPE_THETA = 10000.0
ALPHA = (2.0 * DEPTH) ** 0.25
BETA = (8.0 * DEPTH) ** -0.25
LN_EPS = 1e-5
RMS_EPS = 1e-5

DQ = DIFF_HEADS * 2 * HEAD_DIM
DK = DIFF_HEADS * 2 * HEAD_DIM
DV = DIFF_HEADS * 2 * HEAD_DIM
SQ = SWA_Q_HEADS * HEAD_DIM
SK = SWA_KV_HEADS * HEAD_DIM
SV = SWA_KV_HEADS * HEAD_DIM
IN_COLS = DQ + DK + DV + SQ + SK + SV

kernel_name = "hymba_diffattn_swa_macaron_deepnorm_encoder"


def layer_norm(x, g, b):
    xf = x.astype(jnp.float32)
    mu = jnp.mean(xf, axis=-1, keepdims=True)
    var = jnp.mean(jnp.square(xf - mu), axis=-1, keepdims=True)
    y = (xf - mu) * lax.rsqrt(var + LN_EPS) * g.astype(jnp.float32) + b.astype(jnp.float32)
    return y.astype(x.dtype)


def swiglu(x, w_in, w_out):
    h = x @ w_in
    gate, up = h[..., :D_FF], h[..., D_FF:]
    return (jax.nn.silu(gate) * up) @ w_out


def rope_tables(positions):
    inv = jnp.power(ROPE_THETA, -jnp.arange(0, HEAD_DIM, 2, dtype=jnp.float32) / HEAD_DIM)
    ang = positions.astype(jnp.float32)[..., None] * inv
    ang = jnp.concatenate([ang, ang], axis=-1)[:, :, None, :]
    return jnp.cos(ang), jnp.sin(ang)


def apply_rope(t, cos, sin):
    half = HEAD_DIM // 2
    t1, t2 = t[..., :half], t[..., half:]
    rot = jnp.concatenate([-t2, t1], axis=-1)
    return (t * cos.astype(t.dtype) + rot * sin.astype(t.dtype)).astype(t.dtype)


def diff_attention(q, k, v, lam, subln_g, lambda_init):
    B, S = q.shape[0], q.shape[1]
    nb = S // BLOCK
    scale = HEAD_DIM ** -0.5
    qb = q.reshape(B, nb, BLOCK, DIFF_HEADS, 2, HEAD_DIM).transpose(1, 0, 2, 3, 4, 5)

    def one_block(q_blk):
        s = jnp.einsum('bqhcd,bkhcd->bhcqk', q_blk, k).astype(jnp.float32) * scale
        p = jax.nn.softmax(s, axis=-1)
        a = p[:, :, 0] - lam * p[:, :, 1]
        return jnp.einsum('bhqk,bkhe->bqhe', a.astype(v.dtype), v)

    o = lax.map(one_block, qb)
    o = o.transpose(1, 0, 2, 3, 4).reshape(B, S, DIFF_HEADS, 2 * HEAD_DIM)
    of = o.astype(jnp.float32)
    of = of * lax.rsqrt(jnp.mean(jnp.square(of), axis=-1, keepdims=True) + RMS_EPS)
    of = of * subln_g.astype(jnp.float32) * (1.0 - lambda_init)
    return of.astype(v.dtype).reshape(B, S, DIFF_WIDTH)


def band_mask(S):
    nb = S // BLOCK
    qpos = jnp.arange(nb)[:, None, None] * BLOCK + jnp.arange(BLOCK)[None, :, None]
    kpos = (jnp.arange(nb)[:, None, None] - 1) * BLOCK + jnp.arange(3 * BLOCK)[None, None, :]
    return (jnp.abs(qpos - kpos) <= WINDOW) & (kpos >= 0) & (kpos < S)


def swa_sink_attention(q, k, v, sink, mask):
    B, S = q.shape[0], q.shape[1]
    nb = S // BLOCK
    scale = HEAD_DIM ** -0.5
    qb = q.reshape(B, nb, BLOCK, SWA_KV_HEADS, SWA_GROUP, HEAD_DIM)

    def band(t):
        tp = jnp.pad(t, ((0, 0), (BLOCK, BLOCK), (0, 0), (0, 0)))
        tp = tp.reshape(B, nb + 2, BLOCK, SWA_KV_HEADS, HEAD_DIM)
        return jnp.concatenate([tp[:, :-2], tp[:, 1:-1], tp[:, 2:]], axis=2)

    kb, vb = band(k), band(v)
    s = jnp.einsum('bnqhgd,bnkhd->bnhgqk', qb, kb).astype(jnp.float32) * scale
    s = jnp.where(mask[None, :, None, None, :, :], s, -jnp.inf)
    sk = sink.astype(jnp.float32).reshape(1, 1, SWA_KV_HEADS, SWA_GROUP, 1, 1)
    m = jnp.maximum(jnp.max(s, axis=-1, keepdims=True), sk)
    e = jnp.exp(s - m)
    p = e / (jnp.sum(e, axis=-1, keepdims=True) + jnp.exp(sk - m))
    o = jnp.einsum('bnhgqk,bnkhd->bnqhgd', p.astype(v.dtype), vb)
    return o.reshape(B, S, SWA_WIDTH)


def hybrid_mixer(x, cos, sin, mask, w_in, w_out, lam_vec, subln_g, sink, lambda_init):
    B, S = x.shape[0], x.shape[1]
    h = x @ w_in
    o0 = 0
    dq = h[..., o0:o0 + DQ]; o0 += DQ
    dk = h[..., o0:o0 + DK]; o0 += DK
    dv = h[..., o0:o0 + DV]; o0 += DV
    sq = h[..., o0:o0 + SQ]; o0 += SQ
    sk = h[..., o0:o0 + SK]; o0 += SK
    sv = h[..., o0:o0 + SV]

    dq = apply_rope(dq.reshape(B, S, 2 * DIFF_HEADS, HEAD_DIM), cos, sin)
    dk = apply_rope(dk.reshape(B, S, 2 * DIFF_HEADS, HEAD_DIM), cos, sin)
    dq = dq.reshape(B, S, DIFF_HEADS, 2, HEAD_DIM)
    dk = dk.reshape(B, S, DIFF_HEADS, 2, HEAD_DIM)
    dv = dv.reshape(B, S, DIFF_HEADS, 2 * HEAD_DIM)
    lv = lam_vec.astype(jnp.float32)
    lam = jnp.exp(jnp.sum(lv[0] * lv[1])) - jnp.exp(jnp.sum(lv[2] * lv[3])) + lambda_init
    y_diff = diff_attention(dq, dk, dv, lam, subln_g, lambda_init)

    sq = apply_rope(sq.reshape(B, S, SWA_Q_HEADS, HEAD_DIM), cos, sin)
    sk = apply_rope(sk.reshape(B, S, SWA_KV_HEADS, HEAD_DIM), cos, sin)
    sv = sv.reshape(B, S, SWA_KV_HEADS, HEAD_DIM)
    y_swa = swa_sink_attention(sq, sk, sv, sink, mask)

    return jnp.concatenate([y_diff, y_swa], axis=-1) @ w_out


def setup_inputs(seed: int = 0) -> dict:
    key = jax.random.key(seed)
    ks = jax.random.split(key, 14)
    f32 = jnp.float32
    x = jax.random.normal(ks[0], (BATCH, SEQ, D_MODEL), f32)
    positions = jnp.broadcast_to(jnp.arange(SEQ, dtype=jnp.int32)[None, :], (BATCH, SEQ))
    col_scale = jnp.concatenate([
        jnp.ones((DQ + DK,), f32), jnp.full((DV,), BETA, f32),
        jnp.ones((SQ + SK,), f32), jnp.full((SV,), BETA, f32)])
    w_in = jax.random.normal(ks[1], (DEPTH, D_MODEL, IN_COLS), f32) * (D_MODEL ** -0.5) * col_scale
    w_out = jax.random.normal(ks[2], (DEPTH, MIX_WIDTH, D_MODEL), f32) * (MIX_WIDTH ** -0.5) * BETA
    diff_lambda = jax.random.normal(ks[3], (DEPTH, 4, HEAD_DIM), f32) * 0.1
    diff_subln_g = 1.0 + 0.02 * jax.random.normal(ks[4], (DEPTH, 2 * HEAD_DIM), f32)
    swa_sink = 0.5 * jax.random.normal(ks[5], (DEPTH, SWA_Q_HEADS), f32)
    ffn1_w_in = jax.random.normal(ks[6], (DEPTH, D_MODEL, 2 * D_FF), f32) * (D_MODEL ** -0.5)
    ffn1_w_out = jax.random.normal(ks[7], (DEPTH, D_FF, D_MODEL), f32) * (D_FF ** -0.5) * BETA
    ffn2_w_in = jax.random.normal(ks[8], (DEPTH, D_MODEL, 2 * D_FF), f32) * (D_MODEL ** -0.5)
    ffn2_w_out = jax.random.normal(ks[9], (DEPTH, D_FF, D_MODEL), f32) * (D_FF ** -0.5) * BETA
    ln_g = 1.0 + 0.02 * jax.random.normal(ks[10], (DEPTH, 3, D_MODEL), f32)
    ln_b = 0.02 * jax.random.normal(ks[11], (DEPTH, 3, D_MODEL), f32)
    return {"x": x, "positions": positions, "w_in": w_in, "w_out": w_out,
            "diff_lambda": diff_lambda, "diff_subln_g": diff_subln_g, "swa_sink": swa_sink,
            "ffn1_w_in": ffn1_w_in, "ffn1_w_out": ffn1_w_out,
            "ffn2_w_in": ffn2_w_in, "ffn2_w_out": ffn2_w_out,
            "ln_g": ln_g, "ln_b": ln_b}


def reference(x, positions, w_in, w_out, diff_lambda, diff_subln_g, swa_sink,
              ffn1_w_in, ffn1_w_out, ffn2_w_in, ffn2_w_out, ln_g, ln_b):
    S = x.shape[1]
    cos, sin = rope_tables(positions)
    mask = band_mask(S)
    for l in range(DEPTH):
        lambda_init = 0.8 - 0.6 * math.exp(-0.3 * l)
        x = layer_norm(ALPHA * x + 0.5 * swiglu(x, ffn1_w_in[l], ffn1_w_out[l]), ln_g[l, 0], ln_b[l, 0])
        mix = hybrid_mixer(x, cos, sin, mask, w_in[l], w_out[l], diff_lambda[l],
                           diff_subln_g[l], swa_sink[l], lambda_init)
        x = layer_norm(ALPHA * x + mix, ln_g[l, 1], ln_b[l, 1])
        x = layer_norm(ALPHA * x + 0.5 * swiglu(x, ffn2_w_in[l], ffn2_w_out[l]), ln_g[l, 2], ln_b[l, 2])
    return x
```

```python
import functools
import math

import numpy as np
import jax
import jax.numpy as jnp
from jax import lax
from jax.experimental import pallas as pl
from jax.experimental.pallas import tpu as pltpu

D_MODEL = 1024
DEPTH = 4
HEAD_DIM = 64
HALF = HEAD_DIM // 2
DIFF_HEADS = 4
SWA_Q_HEADS = 8
SWA_KV_HEADS = 2
SWA_GROUP = SWA_Q_HEADS // SWA_KV_HEADS
BLOCK = 128
D_FF = 2816
ROPE_THETA = 10000.0
ALPHA = (2.0 * DEPTH) ** 0.25
LN_EPS = 1e-5
RMS_EPS = 1e-5
LOG2E = 1.4426950408889634
Q_SCALE = (HEAD_DIM ** -0.5) * LOG2E
NEG = -1e30

LANES = 128
DQ = DK = DV = 512
SQ = 512
SK = SV = 128
IN_COLS = DQ + DK + DV + SQ + SK + SV
ROPE_COLS = DQ + DK + SQ + SK

VMEM_LIMIT = 56 * 1024 * 1024

BF16 = jnp.bfloat16
F32 = jnp.float32


def _in_proj_perm():
    def group(col_a, col_b):
        lo = np.arange(HALF)
        return np.concatenate([col_a + lo, col_b + lo, col_a + HALF + lo, col_b + HALF + lo])

    cols = []
    for base in (0, DQ):
        for h in range(DIFF_HEADS):
            cols.append(group(base + h * 128, base + h * 128 + HEAD_DIM))
    sq0 = DQ + DK + DV
    for c in range(SWA_GROUP):
        cols.append(group(sq0 + c * HEAD_DIM, sq0 + (SWA_GROUP + c) * HEAD_DIM))
    sk0 = sq0 + SQ
    cols.append(group(sk0, sk0 + HEAD_DIM))
    cols.append(np.arange(DQ + DK, DQ + DK + DV))
    cols.append(np.arange(sk0 + SK, sk0 + SK + SV))
    return np.concatenate(cols)


def _out_proj_perm():
    rows = [np.arange(512)]
    for c in range(SWA_GROUP):
        rows.append(512 + c * HEAD_DIM + np.arange(HEAD_DIM))
        rows.append(512 + (SWA_GROUP + c) * HEAD_DIM + np.arange(HEAD_DIM))
    return np.concatenate(rows)


_IN_PERM = _in_proj_perm()
_OUT_PERM = _out_proj_perm()


def _layer_norm(z, g, b):
    mu = jnp.mean(z, axis=-1, keepdims=True)
    zc = z - mu
    var = jnp.mean(zc * zc, axis=-1, keepdims=True)
    return zc * lax.rsqrt(var + LN_EPS) * g + b


def _dot(a, b):
    return jnp.dot(a, b, preferred_element_type=F32)


def _rope_table_kernel(pos_ref, inv_ref, sign_ref, cos_ref, sin_ref):
    ang = pos_ref[...] * inv_ref[...]
    cos_ref[...] = jnp.cos(ang)
    sin_ref[...] = jnp.sin(ang) * sign_ref[...]


def _rope_tables(positions):
    n = positions.size
    inv = jnp.power(ROPE_THETA, -jnp.arange(0, HEAD_DIM, 2, dtype=F32) / HEAD_DIM)
    inv128 = jnp.tile(inv, LANES // HALF).reshape(1, LANES)
    sign = jnp.where(jnp.arange(LANES) < LANES // 2, -1.0, 1.0).astype(F32).reshape(1, LANES)
    pos = jnp.broadcast_to(positions.reshape(n, 1).astype(F32), (n, LANES))
    tm = min(n, 2048)
    row = pl.BlockSpec((tm, LANES), lambda i: (i, 0))
    vec = pl.BlockSpec((1, LANES), lambda i: (0, 0))
    return pl.pallas_call(
        _rope_table_kernel,
        grid=(n // tm,),
        in_specs=[row, vec, vec],
        out_specs=[row, row],
        out_shape=[jax.ShapeDtypeStruct((n, LANES), F32)] * 2,
        name="rope_tables",
    )(pos, inv128, sign)


FFN_CHUNK = 256


def _ffn_kernel(x_ref, wi_ref, wo_ref, g_ref, b_ref, o_ref, act_ref):
    x = x_ref[...]
    xb = x.astype(BF16)
    for c in range(D_FF // FFN_CHUNK):
        lo = c * FFN_CHUNK
        gate = _dot(xb, wi_ref[:, lo:lo + FFN_CHUNK])
        up = _dot(xb, wi_ref[:, D_FF + lo:D_FF + lo + FFN_CHUNK])
        act = gate * (1.0 / (1.0 + jnp.exp(-gate))) * up
        act_ref[:, lo:lo + FFN_CHUNK] = act.astype(BF16)
    y = _dot(act_ref[...], wo_ref[...])
    o_ref[...] = _layer_norm(ALPHA * x + 0.5 * y, g_ref[...], b_ref[...])


def _ffn_ln(x, w_in, w_out, g, b, tm):
    n = x.shape[0]
    const = lambda shape: pl.BlockSpec(shape, lambda i: (0, 0), pipeline_mode=pl.Buffered(1))
    row = pl.BlockSpec((tm, D_MODEL), lambda i: (i, 0))
    return pl.pallas_call(
        _ffn_kernel,
        grid=(n // tm,),
        in_specs=[row, const((D_MODEL, 2 * D_FF)), const((D_FF, D_MODEL)),
                  const((1, D_MODEL)), const((1, D_MODEL))],
        out_specs=row,
        out_shape=jax.ShapeDtypeStruct((n, D_MODEL), F32),
        scratch_shapes=[pltpu.VMEM((tm, D_FF), BF16)],
        compiler_params=pltpu.CompilerParams(
            dimension_semantics=("parallel",), vmem_limit_bytes=VMEM_LIMIT),
        name="ffn_ln",
    )(x, w_in, w_out, g, b)


def _proj_kernel(x_ref, w_ref, cos_ref, sin_ref,
                 qd_ref, kd_ref, vd_ref, qs_ref, ks_ref, vs_ref, *, tm):
    xb = x_ref[...].astype(BF16)
    cos = cos_ref[...]
    sin = sin_ref[...]

    def rope(lo, scale):
        t = _dot(xb, w_ref[:, lo:lo + LANES])
        r = t * cos + pltpu.roll(t, LANES // 2, 1) * sin
        return r * scale if scale != 1.0 else r

    def put_transposed(ref, col, t):
        for blk in range(tm // BLOCK):
            ref[0, blk, col:col + LANES, :] = t[blk * BLOCK:(blk + 1) * BLOCK, :].T.astype(BF16)

    for j in range(DQ // LANES):
        put_transposed(qd_ref, j * LANES, rope(j * LANES, Q_SCALE))
    for j in range(DK // LANES):
        kd_ref[:, j * LANES:(j + 1) * LANES] = rope(DQ + j * LANES, 1.0).astype(BF16)
    for j in range(SQ // LANES):
        put_transposed(qs_ref, j * LANES, rope(DQ + DK + j * LANES, Q_SCALE))
    ks_ref[...] = rope(DQ + DK + SQ, 1.0).astype(BF16)
    for j in range(DV // LANES):
        lo = ROPE_COLS + j * LANES
        put_transposed(vd_ref, j * LANES, _dot(xb, w_ref[:, lo:lo + LANES]))
    put_transposed(vs_ref, 0, _dot(xb, w_ref[:, ROPE_COLS + DV:ROPE_COLS + DV + SV]))


def _in_proj(x, w, cos, sin, batch, seq, tm):
    n = x.shape[0]
    nq = seq // BLOCK
    tpb = seq // tm
    sub = tm // BLOCK
    const = lambda shape: pl.BlockSpec(shape, lambda i: (0, 0), pipeline_mode=pl.Buffered(1))
    row = lambda width: pl.BlockSpec((tm, width), lambda i: (i, 0))
    tblk = lambda rows: pl.BlockSpec((1, sub, rows, BLOCK), lambda i: (i // tpb, i % tpb, 0, 0))
    tshape = lambda rows: jax.ShapeDtypeStruct((batch, nq, rows, BLOCK), BF16)
    return pl.pallas_call(
        functools.partial(_proj_kernel, tm=tm),
        grid=(n // tm,),
        in_specs=[row(D_MODEL), const((D_MODEL, IN_COLS)), row(LANES), row(LANES)],
        out_specs=[tblk(DQ), row(DK), tblk(DV), tblk(SQ), row(SK), tblk(SV)],
        out_shape=[tshape(DQ), jax.ShapeDtypeStruct((n, DK), BF16), tshape(DV),
                   tshape(SQ), jax.ShapeDtypeStruct((n, SK), BF16), tshape(SV)],
        compiler_params=pltpu.CompilerParams(
            dimension_semantics=("parallel",), vmem_limit_bytes=VMEM_LIMIT),
        name="in_proj_rope",
    )(x, w, cos, sin)


def _head_select_mask():
    r = lax.broadcasted_iota(jnp.int32, (LANES, 1), 0)
    return (r % HEAD_DIM) < HALF


def _pair_queries(qt, sel0):
    zero = jnp.zeros_like(qt)
    return jnp.concatenate([jnp.where(sel0, qt, zero), jnp.where(sel0, zero, qt)], axis=1)


DIFF_KV_TILE = 512


def _diff_attn_kernel(q_ref, k_ref, v_ref, lam_ref, prm_ref, y_ref, *, seq):
    lv = lam_ref[...]
    a1 = jnp.sum(lv[0:1, :] * lv[1:2, :], axis=-1, keepdims=True)
    a2 = jnp.sum(lv[2:3, :] * lv[3:4, :], axis=-1, keepdims=True)
    lam = jnp.exp(a1) - jnp.exp(a2) + prm_ref[2:3, 0:1]
    gain = prm_ref[0:1, :]
    one_minus_init = prm_ref[1:2, :]
    sel0 = _head_select_mask()
    tk = min(DIFF_KV_TILE, seq)

    def q_block(i, carry):
        qm = _pair_queries(q_ref[0, i], sel0)
        m = jnp.full((1, 2 * BLOCK), -jnp.inf, F32)
        l = jnp.zeros((1, 2 * BLOCK), F32)
        o = jnp.zeros((LANES, 2 * BLOCK), F32)
        for j in range(seq // tk):
            s = _dot(k_ref[j * tk:(j + 1) * tk, :], qm)
            m_new = jnp.maximum(m, jnp.max(s, axis=0, keepdims=True))
            alpha = jnp.exp2(m - m_new)
            p = jnp.exp2(s - m_new)
            l = alpha * l + jnp.sum(p, axis=0, keepdims=True)
            vt = jnp.concatenate(
                [v_ref[0, j * (tk // BLOCK) + t] for t in range(tk // BLOCK)], axis=1)
            o = alpha * o + _dot(vt, p.astype(BF16))
            m = m_new
        on = o * (1.0 / l)
        d = (on[:, :BLOCK] - lam * on[:, BLOCK:]).T
        ms = jnp.mean(d * d, axis=-1, keepdims=True)
        y = d * lax.rsqrt(ms + RMS_EPS) * gain * one_minus_init
        y_ref[pl.ds(pl.multiple_of(i * BLOCK, BLOCK), BLOCK), :] = y.astype(BF16)
        return carry

    lax.fori_loop(0, seq // BLOCK, q_block, 0)


def _diff_attn(qd, kd, vd, lam_vec, prm, batch, seq):
    nq = seq // BLOCK
    tspec = pl.BlockSpec((1, nq, LANES, BLOCK), lambda b, h: (b, 0, h, 0))
    kspec = pl.BlockSpec((seq, LANES), lambda b, h: (b, h))
    small = lambda shape: pl.BlockSpec(shape, lambda b, h: (0, 0))
    return pl.pallas_call(
        functools.partial(_diff_attn_kernel, seq=seq),
        grid=(batch, DIFF_HEADS),
        in_specs=[tspec, kspec, tspec, small((4, HEAD_DIM)), small((8, LANES))],
        out_specs=pl.BlockSpec((seq, LANES), lambda b, h: (b, h)),
        out_shape=jax.ShapeDtypeStruct((batch * seq, DV), BF16),
        compiler_params=pltpu.CompilerParams(
            dimension_semantics=("parallel", "parallel"), vmem_limit_bytes=VMEM_LIMIT),
        name="diff_attn",
    )(qd, kd, vd, lam_vec, prm)


def _swa_kernel(q_ref, k_ref, v_ref, sink_ref, y_ref, *, seq):
    nq = seq // BLOCK
    sel0 = _head_select_mask()
    r = lax.broadcasted_iota(jnp.int32, (BLOCK, BLOCK), 0)
    c = lax.broadcasted_iota(jnp.int32, (BLOCK, BLOCK), 1)
    prev_band = jnp.where(c <= r, 0.0, NEG).astype(F32)
    next_band = jnp.where(r <= c, 0.0, NEG).astype(F32)
    zeros = jnp.zeros((BLOCK, BLOCK), F32)

    def q_block(n, carry):
        n_prev = jnp.maximum(n - 1, 0)
        n_next = jnp.minimum(n + 1, nq - 1)
        edge_prev = jnp.where(n >= 1, 0.0, NEG).astype(F32)
        edge_next = jnp.where(n <= nq - 2, 0.0, NEG).astype(F32)
        bias = jnp.concatenate([prev_band + edge_prev, zeros, next_band + edge_next], axis=0)
        bias = jnp.concatenate([bias, bias], axis=1)
        rows = lambda blk: pl.ds(pl.multiple_of(blk * BLOCK, BLOCK), BLOCK)
        kb = jnp.concatenate([k_ref[rows(n_prev), :], k_ref[rows(n), :], k_ref[rows(n_next), :]],
                             axis=0)
        vb = jnp.concatenate([v_ref[0, n_prev], v_ref[0, n], v_ref[0, n_next]], axis=1)
        for g in range(SWA_GROUP):
            qm = _pair_queries(q_ref[0, n, g * LANES:(g + 1) * LANES, :], sel0)
            s = _dot(kb, qm) + bias
            sink = sink_ref[:, g * 2 * BLOCK:(g + 1) * 2 * BLOCK]
            m = jnp.maximum(jnp.max(s, axis=0, keepdims=True), sink)
            e = jnp.exp2(s - m)
            den = jnp.sum(e, axis=0, keepdims=True) + jnp.exp2(sink - m)
            on = _dot(vb, e.astype(BF16)) * (1.0 / den)
            yt = jnp.concatenate([on[:HEAD_DIM, :BLOCK], on[HEAD_DIM:, BLOCK:]], axis=0)
            y_ref[rows(n), g * LANES:(g + 1) * LANES] = yt.T.astype(BF16)
        return carry

    lax.fori_loop(0, nq, q_block, 0)


def _swa_attn(qs, ks, vs, sink_vec, batch, seq):
    nq = seq // BLOCK
    return pl.pallas_call(
        functools.partial(_swa_kernel, seq=seq),
        grid=(batch,),
        in_specs=[pl.BlockSpec((1, nq, SQ, BLOCK), lambda b: (b, 0, 0, 0)),
                  pl.BlockSpec((seq, SK), lambda b: (b, 0)),
                  pl.BlockSpec((1, nq, SV, BLOCK), lambda b: (b, 0, 0, 0)),
                  pl.BlockSpec((1, SWA_Q_HEADS * BLOCK), lambda b: (0, 0))],
        out_specs=pl.BlockSpec((seq, SQ), lambda b: (b, 0)),
        out_shape=jax.ShapeDtypeStruct((batch * seq, SQ), BF16),
        compiler_params=pltpu.CompilerParams(
            dimension_semantics=("parallel",), vmem_limit_bytes=VMEM_LIMIT),
        name="swa_attn",
    )(qs, ks, vs, sink_vec)


def _out_proj_kernel(x_ref, yd_ref, ys_ref, w_ref, g_ref, b_ref, o_ref):
    mix = _dot(yd_ref[...], w_ref[:DV, :]) + _dot(ys_ref[...], w_ref[DV:, :])
    o_ref[...] = _layer_norm(ALPHA * x_ref[...] + mix, g_ref[...], b_ref[...])


def _out_proj_ln(x, yd, ys, w, g, b, tm):
    n = x.shape[0]
    const = lambda shape: pl.BlockSpec(shape, lambda i: (0, 0), pipeline_mode=pl.Buffered(1))
    row = lambda width: pl.BlockSpec((tm, width), lambda i: (i, 0))
    return pl.pallas_call(
        _out_proj_kernel,
        grid=(n // tm,),
        in_specs=[row(D_MODEL), row(DV), row(SQ), const((DV + SQ, D_MODEL)),
                  const((1, D_MODEL)), const((1, D_MODEL))],
        out_specs=row(D_MODEL),
        out_shape=jax.ShapeDtypeStruct((n, D_MODEL), F32),
        compiler_params=pltpu.CompilerParams(
            dimension_semantics=("parallel",), vmem_limit_bytes=VMEM_LIMIT),
        name="out_proj_ln",
    )(x, yd, ys, w, g, b)


def _token_tile(n, want):
    return want if n % want == 0 else BLOCK


def kernel(x, positions, w_in, w_out, diff_lambda, diff_subln_g, swa_sink,
           ffn1_w_in, ffn1_w_out, ffn2_w_in, ffn2_w_out, ln_g, ln_b):
    batch, seq, _ = x.shape
    n = batch * seq
    tm = _token_tile(seq, 512)
    h = x.reshape(n, D_MODEL)
    cos, sin = _rope_tables(positions)

    w_in_p = jnp.take(w_in, jnp.asarray(_IN_PERM), axis=2).astype(BF16)
    w_out_p = jnp.take(w_out, jnp.asarray(_OUT_PERM), axis=1).astype(BF16)
    sink = swa_sink.reshape(DEPTH, SWA_KV_HEADS, SWA_GROUP).transpose(0, 2, 1) * LOG2E
    sink_vec = jnp.repeat(sink.reshape(DEPTH, 1, SWA_Q_HEADS), BLOCK, axis=2)

    for l in range(DEPTH):
        lambda_init = 0.8 - 0.6 * math.exp(-0.3 * l)
        g = lambda i: ln_g[l, i].reshape(1, D_MODEL)
        b = lambda i: ln_b[l, i].reshape(1, D_MODEL)
        prm = jnp.zeros((8, LANES), F32)
        prm = prm.at[0].set(diff_subln_g[l]).at[1].set(1.0 - lambda_init).at[2].set(lambda_init)

        h = _ffn_ln(h, ffn1_w_in[l].astype(BF16), ffn1_w_out[l].astype(BF16), g(0), b(0), tm)
        qd, kd, vd, qs, ks, vs = _in_proj(h, w_in_p[l], cos, sin, batch, seq, tm)
        yd = _diff_attn(qd, kd, vd, diff_lambda[l], prm, batch, seq)
        ys = _swa_attn(qs, ks, vs, sink_vec[l], batch, seq)
        h = _out_proj_ln(h, yd, ys, w_out_p[l], g(1), b(1), tm)
        h = _ffn_ln(h, ffn2_w_in[l].astype(BF16), ffn2_w_out[l].astype(BF16), g(2), b(2), tm)
    return h.reshape(batch, seq, D_MODEL)
```

```python
import functools
import math

import numpy as np
import jax
import jax.numpy as jnp
from jax import lax
from jax.experimental import pallas as pl
from jax.experimental.pallas import tpu as pltpu

D_MODEL = 1024
DEPTH = 4
HEAD_DIM = 64
HALF = HEAD_DIM // 2
DIFF_HEADS = 4
SWA_Q_HEADS = 8
SWA_KV_HEADS = 2
SWA_GROUP = SWA_Q_HEADS // SWA_KV_HEADS
BLOCK = 128
D_FF = 2816
ROPE_THETA = 10000.0
ALPHA = (2.0 * DEPTH) ** 0.25
LN_EPS = 1e-5
RMS_EPS = 1e-5
LOG2E = 1.4426950408889634
Q_SCALE = (HEAD_DIM ** -0.5) * LOG2E
NEG = -1e30

LANES = 128
DQ = DK = DV = 512
SQ = 512
SK = SV = 128
IN_COLS = DQ + DK + DV + SQ + SK + SV
ROPE_COLS = DQ + DK + SQ + SK

VMEM_LIMIT = 56 * 1024 * 1024

BF16 = jnp.bfloat16
F32 = jnp.float32


def _in_proj_perm():
    def group(col_a, col_b):
        lo = np.arange(HALF)
        return np.concatenate([col_a + lo, col_b + lo, col_a + HALF + lo, col_b + HALF + lo])

    cols = []
    for base in (0, DQ):
        for h in range(DIFF_HEADS):
            cols.append(group(base + h * 128, base + h * 128 + HEAD_DIM))
    sq0 = DQ + DK + DV
    for c in range(SWA_GROUP):
        cols.append(group(sq0 + c * HEAD_DIM, sq0 + (SWA_GROUP + c) * HEAD_DIM))
    sk0 = sq0 + SQ
    cols.append(group(sk0, sk0 + HEAD_DIM))
    cols.append(np.arange(DQ + DK, DQ + DK + DV))
    cols.append(np.arange(sk0 + SK, sk0 + SK + SV))
    return np.concatenate(cols)


def _out_proj_perm():
    rows = [np.arange(512)]
    for c in range(SWA_GROUP):
        rows.append(512 + c * HEAD_DIM + np.arange(HEAD_DIM))
        rows.append(512 + (SWA_GROUP + c) * HEAD_DIM + np.arange(HEAD_DIM))
    return np.concatenate(rows)


_IN_PERM = _in_proj_perm()
_OUT_PERM = _out_proj_perm()


def _layer_norm(z, g, b):
    mu = jnp.mean(z, axis=-1, keepdims=True)
    zc = z - mu
    var = jnp.mean(zc * zc, axis=-1, keepdims=True)
    return zc * lax.rsqrt(var + LN_EPS) * g + b


def _dot(a, b):
    return jnp.dot(a, b, preferred_element_type=F32)


def _rope_table_kernel(pos_ref, inv_ref, sign_ref, cos_ref, sin_ref):
    ang = pos_ref[...] * inv_ref[...]
    cos_ref[...] = jnp.cos(ang)
    sin_ref[...] = jnp.sin(ang) * sign_ref[...]


def _rope_tables(positions):
    n = positions.size
    inv = jnp.power(ROPE_THETA, -jnp.arange(0, HEAD_DIM, 2, dtype=F32) / HEAD_DIM)
    inv128 = jnp.tile(inv, LANES // HALF).reshape(1, LANES)
    sign = jnp.where(jnp.arange(LANES) < LANES // 2, -1.0, 1.0).astype(F32).reshape(1, LANES)
    pos = jnp.broadcast_to(positions.reshape(n, 1).astype(F32), (n, LANES))
    tm = min(n, 2048)
    row = pl.BlockSpec((tm, LANES), lambda i: (i, 0))
    vec = pl.BlockSpec((1, LANES), lambda i: (0, 0))
    return pl.pallas_call(
        _rope_table_kernel,
        grid=(n // tm,),
        in_specs=[row, vec, vec],
        out_specs=[row, row],
        out_shape=[jax.ShapeDtypeStruct((n, LANES), F32)] * 2,
        name="rope_tables",
    )(pos, inv128, sign)


FFN_CHUNK = 256


def _ffn_residual_ln(x, wi_ref, wo_ref, g_ref, b_ref, act_ref):
    xb = x.astype(BF16)
    for c in range(D_FF // FFN_CHUNK):
        lo = c * FFN_CHUNK
        gate = _dot(xb, wi_ref[:, lo:lo + FFN_CHUNK])
        up = _dot(xb, wi_ref[:, D_FF + lo:D_FF + lo + FFN_CHUNK])
        act = gate * (1.0 / (1.0 + jnp.exp(-gate))) * up
        act_ref[:, lo:lo + FFN_CHUNK] = act.astype(BF16)
    y = _dot(act_ref[...], wo_ref[...])
    return _layer_norm(ALPHA * x + 0.5 * y, g_ref[...], b_ref[...])


def _ffn_kernel(x_ref, wi_ref, wo_ref, g_ref, b_ref, o_ref, act_ref):
    o_ref[...] = _ffn_residual_ln(x_ref[...], wi_ref, wo_ref, g_ref, b_ref, act_ref)


def _const_spec(shape):
    return pl.BlockSpec(shape, lambda i: (0, 0), pipeline_mode=pl.Buffered(1))


def _ffn_ln(x, w_in, w_out, g, b, tm):
    n = x.shape[0]
    row = pl.BlockSpec((tm, D_MODEL), lambda i: (i, 0))
    return pl.pallas_call(
        _ffn_kernel,
        grid=(n // tm,),
        in_specs=[row, _const_spec((D_MODEL, 2 * D_FF)), _const_spec((D_FF, D_MODEL)),
                  _const_spec((1, D_MODEL)), _const_spec((1, D_MODEL))],
        out_specs=row,
        out_shape=jax.ShapeDtypeStruct((n, D_MODEL), F32),
        scratch_shapes=[pltpu.VMEM((tm, D_FF), BF16)],
        compiler_params=pltpu.CompilerParams(
            dimension_semantics=("parallel",), vmem_limit_bytes=VMEM_LIMIT),
        name="ffn_ln",
    )(x, w_in, w_out, g, b)


def _mix_ffn_kernel(x_ref, yd_ref, ys_ref, wm_ref, gm_ref, bm_ref,
                    wi_ref, wo_ref, g_ref, b_ref, o_ref, act_ref):
    mix = _dot(yd_ref[...], wm_ref[:DV, :]) + _dot(ys_ref[...], wm_ref[DV:, :])
    x = _layer_norm(ALPHA * x_ref[...] + mix, gm_ref[...], bm_ref[...])
    o_ref[...] = _ffn_residual_ln(x, wi_ref, wo_ref, g_ref, b_ref, act_ref)


def _mix_ffn_ln(x, yd, ys, w_mix, g_mix, b_mix, w_in, w_out, g, b, tm):
    n = x.shape[0]
    row = lambda width: pl.BlockSpec((tm, width), lambda i: (i, 0))
    vec = _const_spec((1, D_MODEL))
    return pl.pallas_call(
        _mix_ffn_kernel,
        grid=(n // tm,),
        in_specs=[row(D_MODEL), row(DV), row(SQ), _const_spec((DV + SQ, D_MODEL)), vec, vec,
                  _const_spec((D_MODEL, 2 * D_FF)), _const_spec((D_FF, D_MODEL)), vec, vec],
        out_specs=row(D_MODEL),
        out_shape=jax.ShapeDtypeStruct((n, D_MODEL), F32),
        scratch_shapes=[pltpu.VMEM((tm, D_FF), BF16)],
        compiler_params=pltpu.CompilerParams(
            dimension_semantics=("parallel",), vmem_limit_bytes=VMEM_LIMIT),
        name="mix_ffn_ln",
    )(x, yd, ys, w_mix, g_mix, b_mix, w_in, w_out, g, b)


def _proj_kernel(x_ref, w_ref, cos_ref, sin_ref,
                 qd_ref, kd_ref, vd_ref, qs_ref, ks_ref, vs_ref, *, tm):
    xb = x_ref[...].astype(BF16)
    cos = cos_ref[...]
    sin = sin_ref[...]

    def rope(t, scale):
        r = t * cos + pltpu.roll(t, LANES // 2, 1) * sin
        return r * scale if scale != 1.0 else r

    def put_transposed(ref, col, t):
        for blk in range(tm // BLOCK):
            ref[0, blk, col:col + LANES, :] = t[blk * BLOCK:(blk + 1) * BLOCK, :].T.astype(BF16)

    def put_rows(ref, col, t):
        ref[:, col:col + LANES] = t.astype(BF16)

    plan = []
    for j in range(DQ // LANES):
        plan.append((j * LANES, qd_ref, j * LANES, put_transposed, Q_SCALE))
    for j in range(DK // LANES):
        plan.append((DQ + j * LANES, kd_ref, j * LANES, put_rows, 1.0))
    for j in range(SQ // LANES):
        plan.append((DQ + DK + j * LANES, qs_ref, j * LANES, put_transposed, Q_SCALE))
    plan.append((DQ + DK + SQ, ks_ref, 0, put_rows, 1.0))
    for j in range(DV // LANES):
        plan.append((ROPE_COLS + j * LANES, vd_ref, j * LANES, put_transposed, None))
    plan.append((ROPE_COLS + DV, vs_ref, 0, put_transposed, None))

    for first in range(0, len(plan), 2):
        lo = plan[first][0]
        t2 = _dot(xb, w_ref[:, lo:lo + 2 * LANES])
        for half, (_, ref, col, put, scale) in enumerate(plan[first:first + 2]):
            t = t2[:, half * LANES:(half + 1) * LANES]
            put(ref, col, t if scale is None else rope(t, scale))


def _in_proj(x, w, cos, sin, batch, seq, tm):
    n = x.shape[0]
    nq = seq // BLOCK
    tpb = seq // tm
    sub = tm // BLOCK
    const = _const_spec
    row = lambda width: pl.BlockSpec((tm, width), lambda i: (i, 0))
    tblk = lambda rows: pl.BlockSpec((1, sub, rows, BLOCK), lambda i: (i // tpb, i % tpb, 0, 0))
    tshape = lambda rows: jax.ShapeDtypeStruct((batch, nq, rows, BLOCK), BF16)
    return pl.pallas_call(
        functools.partial(_proj_kernel, tm=tm),
        grid=(n // tm,),
        in_specs=[row(D_MODEL), const((D_MODEL, IN_COLS)), row(LANES), row(LANES)],
        out_specs=[tblk(DQ), row(DK), tblk(DV), tblk(SQ), row(SK), tblk(SV)],
        out_shape=[tshape(DQ), jax.ShapeDtypeStruct((n, DK), BF16), tshape(DV),
                   tshape(SQ), jax.ShapeDtypeStruct((n, SK), BF16), tshape(SV)],
        compiler_params=pltpu.CompilerParams(
            dimension_semantics=("parallel",), vmem_limit_bytes=VMEM_LIMIT),
        name="in_proj_rope",
    )(x, w, cos, sin)


def _head_select_mask():
    r = lax.broadcasted_iota(jnp.int32, (LANES, 1), 0)
    return (r % HEAD_DIM) < HALF


def _pair_queries(qt, sel0):
    zero = jnp.zeros_like(qt)
    return jnp.concatenate([jnp.where(sel0, qt, zero), jnp.where(sel0, zero, qt)], axis=1)


DIFF_KV_TILE = 512


def _diff_attn_kernel(q_ref, k_ref, v_ref, lam_ref, prm_ref, y_ref, s0_ref, s1_ref, *, seq):
    lv = lam_ref[...]
    a1 = jnp.sum(lv[0:1, :] * lv[1:2, :], axis=-1, keepdims=True)
    a2 = jnp.sum(lv[2:3, :] * lv[3:4, :], axis=-1, keepdims=True)
    lam = jnp.exp(a1) - jnp.exp(a2) + prm_ref[2:3, 0:1]
    gain = prm_ref[0:1, :]
    one_minus_init = prm_ref[1:2, :]
    sel0 = _head_select_mask()
    tk = min(DIFF_KV_TILE, seq)
    nq = seq // BLOCK
    tiles = [(j * tk, (j + 1) * tk) for j in range(seq // tk)]

    def scores(i, s_ref):
        qm = _pair_queries(q_ref[0, i], sel0)
        m = None
        for lo, hi in tiles:
            s = _dot(k_ref[lo:hi, :], qm)
            s_ref[lo:hi, :] = s
            tile_max = jnp.max(s, axis=0, keepdims=True)
            m = tile_max if m is None else jnp.maximum(m, tile_max)
        return m

    def finish(i, s_ref, m):
        l = jnp.zeros((1, 2 * BLOCK), F32)
        o = jnp.zeros((LANES, 2 * BLOCK), F32)
        for lo, hi in tiles:
            p = jnp.exp2(s_ref[lo:hi, :] - m)
            l = l + jnp.sum(p, axis=0, keepdims=True)
            vt = jnp.concatenate([v_ref[0, t] for t in range(lo // BLOCK, hi // BLOCK)], axis=1)
            o = o + _dot(vt, p.astype(BF16))
        on = o * (1.0 / l)
        d = (on[:, :BLOCK] - lam * on[:, BLOCK:]).T
        ms = jnp.mean(d * d, axis=-1, keepdims=True)
        y = d * lax.rsqrt(ms + RMS_EPS) * gain * one_minus_init
        y_ref[pl.ds(pl.multiple_of(i * BLOCK, BLOCK), BLOCK), :] = y.astype(BF16)

    def block_pair(t, m_even):
        i = 2 * t
        m_odd = scores(i + 1, s1_ref)
        finish(i, s0_ref, m_even)
        m_next = scores(jnp.minimum(i + 2, nq - 1), s0_ref)
        finish(i + 1, s1_ref, m_odd)
        return m_next

    lax.fori_loop(0, nq // 2, block_pair, scores(0, s0_ref))


def _diff_attn(qd, kd, vd, lam_vec, prm, batch, seq):
    nq = seq // BLOCK
    tspec = pl.BlockSpec((1, nq, LANES, BLOCK), lambda b, h: (b, 0, h, 0))
    kspec = pl.BlockSpec((seq, LANES), lambda b, h: (b, h))
    small = lambda shape: pl.BlockSpec(shape, lambda b, h: (0, 0))
    return pl.pallas_call(
        functools.partial(_diff_attn_kernel, seq=seq),
        grid=(batch, DIFF_HEADS),
        in_specs=[tspec, kspec, tspec, small((4, HEAD_DIM)), small((8, LANES))],
        out_specs=pl.BlockSpec((seq, LANES), lambda b, h: (b, h)),
        out_shape=jax.ShapeDtypeStruct((batch * seq, DV), BF16),
        scratch_shapes=[pltpu.VMEM((seq, 2 * BLOCK), F32)] * 2,
        compiler_params=pltpu.CompilerParams(
            dimension_semantics=("parallel", "parallel"), vmem_limit_bytes=VMEM_LIMIT),
        name="diff_attn",
    )(qd, kd, vd, lam_vec, prm)


def _swa_kernel(q_ref, k_ref, v_ref, sink_ref, y_ref, *, seq):
    nq = seq // BLOCK
    sel0 = _head_select_mask()
    r = lax.broadcasted_iota(jnp.int32, (BLOCK, BLOCK), 0)
    c = lax.broadcasted_iota(jnp.int32, (BLOCK, BLOCK), 1)
    prev_band = jnp.where(c <= r, 0.0, NEG).astype(F32)
    next_band = jnp.where(r <= c, 0.0, NEG).astype(F32)
    zeros = jnp.zeros((BLOCK, BLOCK), F32)

    def q_block(n, carry):
        n_prev = jnp.maximum(n - 1, 0)
        n_next = jnp.minimum(n + 1, nq - 1)
        edge_prev = jnp.where(n >= 1, 0.0, NEG).astype(F32)
        edge_next = jnp.where(n <= nq - 2, 0.0, NEG).astype(F32)
        bias = jnp.concatenate([prev_band + edge_prev, zeros, next_band + edge_next], axis=0)
        bias = jnp.concatenate([bias, bias], axis=1)
        rows = lambda blk: pl.ds(pl.multiple_of(blk * BLOCK, BLOCK), BLOCK)
        kb = jnp.concatenate([k_ref[rows(n_prev), :], k_ref[rows(n), :], k_ref[rows(n_next), :]],
                             axis=0)
        vb = jnp.concatenate([v_ref[0, n_prev], v_ref[0, n], v_ref[0, n_next]], axis=1)
        qms = [_pair_queries(q_ref[0, n, g * LANES:(g + 1) * LANES, :], sel0)
               for g in range(SWA_GROUP)]
        scores = [_dot(kb, qm) + bias for qm in qms]
        outs = []
        for g, s in enumerate(scores):
            sink = sink_ref[:, g * 2 * BLOCK:(g + 1) * 2 * BLOCK]
            m = jnp.maximum(jnp.max(s, axis=0, keepdims=True), sink)
            e = jnp.exp2(s - m)
            den = jnp.sum(e, axis=0, keepdims=True) + jnp.exp2(sink - m)
            on = _dot(vb, e.astype(BF16)) * (1.0 / den)
            yt = jnp.concatenate([on[:HEAD_DIM, :BLOCK], on[HEAD_DIM:, BLOCK:]], axis=0)
            outs.append(yt.T.astype(BF16))
        y_ref[rows(n), :] = jnp.concatenate(outs, axis=1)
        return carry

    lax.fori_loop(0, nq, q_block, 0)


def _swa_attn(qs, ks, vs, sink_vec, batch, seq):
    nq = seq // BLOCK
    return pl.pallas_call(
        functools.partial(_swa_kernel, seq=seq),
        grid=(batch,),
        in_specs=[pl.BlockSpec((1, nq, SQ, BLOCK), lambda b: (b, 0, 0, 0)),
                  pl.BlockSpec((seq, SK), lambda b: (b, 0)),
                  pl.BlockSpec((1, nq, SV, BLOCK), lambda b: (b, 0, 0, 0)),
                  pl.BlockSpec((1, SWA_Q_HEADS * BLOCK), lambda b: (0, 0))],
        out_specs=pl.BlockSpec((seq, SQ), lambda b: (b, 0)),
        out_shape=jax.ShapeDtypeStruct((batch * seq, SQ), BF16),
        compiler_params=pltpu.CompilerParams(
            dimension_semantics=("parallel",), vmem_limit_bytes=VMEM_LIMIT),
        name="swa_attn",
    )(qs, ks, vs, sink_vec)


def _token_tile(n, want):
    return want if n % want == 0 else BLOCK


def kernel(x, positions, w_in, w_out, diff_lambda, diff_subln_g, swa_sink,
           ffn1_w_in, ffn1_w_out, ffn2_w_in, ffn2_w_out, ln_g, ln_b):
    batch, seq, _ = x.shape
    n = batch * seq
    tm = _token_tile(seq, 512)
    h = x.reshape(n, D_MODEL)
    cos, sin = _rope_tables(positions)

    w_in_p = jnp.take(w_in, jnp.asarray(_IN_PERM), axis=2).astype(BF16)
    w_out_p = jnp.take(w_out, jnp.asarray(_OUT_PERM), axis=1).astype(BF16)
    sink = swa_sink.reshape(DEPTH, SWA_KV_HEADS, SWA_GROUP).transpose(0, 2, 1) * LOG2E
    sink_vec = jnp.repeat(sink.reshape(DEPTH, 1, SWA_Q_HEADS), BLOCK, axis=2)

    for l in range(DEPTH):
        lambda_init = 0.8 - 0.6 * math.exp(-0.3 * l)
        g = lambda i: ln_g[l, i].reshape(1, D_MODEL)
        b = lambda i: ln_b[l, i].reshape(1, D_MODEL)
        prm = jnp.zeros((8, LANES), F32)
        prm = prm.at[0].set(diff_subln_g[l]).at[1].set(1.0 - lambda_init).at[2].set(lambda_init)

        h = _ffn_ln(h, ffn1_w_in[l].astype(BF16), ffn1_w_out[l].astype(BF16), g(0), b(0), tm)
        qd, kd, vd, qs, ks, vs = _in_proj(h, w_in_p[l], cos, sin, batch, seq, tm)
        yd = _diff_attn(qd, kd, vd, diff_lambda[l], prm, batch, seq)
        ys = _swa_attn(qs, ks, vs, sink_vec[l], batch, seq)
        h = _mix_ffn_ln(h, yd, ys, w_out_p[l], g(1), b(1),
                        ffn2_w_in[l].astype(BF16), ffn2_w_out[l].astype(BF16), g(2), b(2), tm)
    return h.reshape(batch, seq, D_MODEL)
```

```python
import functools
import math

import numpy as np
import jax
import jax.numpy as jnp
from jax import lax
from jax.experimental import pallas as pl
from jax.experimental.pallas import tpu as pltpu

D_MODEL = 1024
DEPTH = 4
HEAD_DIM = 64
HALF = HEAD_DIM // 2
DIFF_HEADS = 4
SWA_Q_HEADS = 8
SWA_KV_HEADS = 2
SWA_GROUP = SWA_Q_HEADS // SWA_KV_HEADS
BLOCK = 128
D_FF = 2816
ROPE_THETA = 10000.0
ALPHA = (2.0 * DEPTH) ** 0.25
LN_EPS = 1e-5
RMS_EPS = 1e-5
LOG2E = 1.4426950408889634
Q_SCALE = (HEAD_DIM ** -0.5) * LOG2E
NEG = -1e30

LANES = 128
DQ = DK = DV = 512
SQ = 512
SK = SV = 128
IN_COLS = DQ + DK + DV + SQ + SK + SV
ROPE_COLS = DQ + DK + SQ + SK

VMEM_LIMIT = 56 * 1024 * 1024

BF16 = jnp.bfloat16
F32 = jnp.float32


def _in_proj_perm():
    def group(col_a, col_b):
        lo = np.arange(HALF)
        return np.concatenate([col_a + lo, col_b + lo, col_a + HALF + lo, col_b + HALF + lo])

    cols = []
    for base in (0, DQ):
        for h in range(DIFF_HEADS):
            cols.append(group(base + h * 128, base + h * 128 + HEAD_DIM))
    sq0 = DQ + DK + DV
    for c in range(SWA_GROUP):
        cols.append(group(sq0 + c * HEAD_DIM, sq0 + (SWA_GROUP + c) * HEAD_DIM))
    sk0 = sq0 + SQ
    cols.append(group(sk0, sk0 + HEAD_DIM))
    cols.append(np.arange(DQ + DK, DQ + DK + DV))
    cols.append(np.arange(sk0 + SK, sk0 + SK + SV))
    return np.concatenate(cols)


def _out_proj_perm():
    rows = [np.arange(512)]
    for c in range(SWA_GROUP):
        rows.append(512 + c * HEAD_DIM + np.arange(HEAD_DIM))
        rows.append(512 + (SWA_GROUP + c) * HEAD_DIM + np.arange(HEAD_DIM))
    return np.concatenate(rows)


_IN_PERM = _in_proj_perm()
_OUT_PERM = _out_proj_perm()


def _layer_norm(z, g, b):
    mu = jnp.mean(z, axis=-1, keepdims=True)
    zc = z - mu
    var = jnp.mean(zc * zc, axis=-1, keepdims=True)
    return zc * lax.rsqrt(var + LN_EPS) * g + b


def _dot(a, b):
    return jnp.dot(a, b, preferred_element_type=F32)


def _rope_table_kernel(pos_ref, inv_ref, sign_ref, cos_ref, sin_ref):
    ang = pos_ref[...] * inv_ref[...]
    cos_ref[...] = jnp.cos(ang)
    sin_ref[...] = jnp.sin(ang) * sign_ref[...]


def _rope_tables(positions):
    n = positions.size
    inv = jnp.power(ROPE_THETA, -jnp.arange(0, HEAD_DIM, 2, dtype=F32) / HEAD_DIM)
    inv128 = jnp.tile(inv, LANES // HALF).reshape(1, LANES)
    sign = jnp.where(jnp.arange(LANES) < LANES // 2, -1.0, 1.0).astype(F32).reshape(1, LANES)
    pos = jnp.broadcast_to(positions.reshape(n, 1).astype(F32), (n, LANES))
    tm = min(n, 2048)
    row = pl.BlockSpec((tm, LANES), lambda i: (i, 0))
    vec = pl.BlockSpec((1, LANES), lambda i: (0, 0))
    return pl.pallas_call(
        _rope_table_kernel,
        grid=(n // tm,),
        in_specs=[row, vec, vec],
        out_specs=[row, row],
        out_shape=[jax.ShapeDtypeStruct((n, LANES), F32)] * 2,
        name="rope_tables",
    )(pos, inv128, sign)


FFN_CHUNK = 256


FFN_OUT_SPLITS = 2


def _ffn_residual_ln(x, wi_ref, wo_ref, g_ref, b_ref, o_ref, act_ref):
    xb = x.astype(BF16)
    for c in range(D_FF // FFN_CHUNK):
        lo = c * FFN_CHUNK
        gate = _dot(xb, wi_ref[:, lo:lo + FFN_CHUNK])
        up = _dot(xb, wi_ref[:, D_FF + lo:D_FF + lo + FFN_CHUNK])
        act = gate * (1.0 / (1.0 + jnp.exp(-gate))) * up
        act_ref[:, lo:lo + FFN_CHUNK] = act.astype(BF16)
    rows = x.shape[0] // FFN_OUT_SPLITS
    for r in range(FFN_OUT_SPLITS):
        sl = slice(r * rows, (r + 1) * rows)
        y = _dot(act_ref[sl, :], wo_ref[...])
        o_ref[sl, :] = _layer_norm(ALPHA * x[sl, :] + 0.5 * y, g_ref[...], b_ref[...])


def _ffn_kernel(x_ref, wi_ref, wo_ref, g_ref, b_ref, o_ref, act_ref):
    _ffn_residual_ln(x_ref[...], wi_ref, wo_ref, g_ref, b_ref, o_ref, act_ref)


def _const_spec(shape):
    return pl.BlockSpec(shape, lambda i: (0, 0), pipeline_mode=pl.Buffered(1))


def _ffn_ln(x, w_in, w_out, g, b, tm):
    n = x.shape[0]
    row = pl.BlockSpec((tm, D_MODEL), lambda i: (i, 0))
    return pl.pallas_call(
        _ffn_kernel,
        grid=(n // tm,),
        in_specs=[row, _const_spec((D_MODEL, 2 * D_FF)), _const_spec((D_FF, D_MODEL)),
                  _const_spec((1, D_MODEL)), _const_spec((1, D_MODEL))],
        out_specs=row,
        out_shape=jax.ShapeDtypeStruct((n, D_MODEL), F32),
        scratch_shapes=[pltpu.VMEM((tm, D_FF), BF16)],
        compiler_params=pltpu.CompilerParams(
            dimension_semantics=("parallel",), vmem_limit_bytes=VMEM_LIMIT),
        name="ffn_ln",
    )(x, w_in, w_out, g, b)


def _mix_ffn_kernel(x_ref, yd_ref, ys_ref, wm_ref, gm_ref, bm_ref,
                    wi_ref, wo_ref, g_ref, b_ref, o_ref, act_ref):
    mix = _dot(yd_ref[...], wm_ref[:DV, :]) + _dot(ys_ref[...], wm_ref[DV:, :])
    x = _layer_norm(ALPHA * x_ref[...] + mix, gm_ref[...], bm_ref[...])
    _ffn_residual_ln(x, wi_ref, wo_ref, g_ref, b_ref, o_ref, act_ref)


def _mix_ffn_ln(x, yd, ys, w_mix, g_mix, b_mix, w_in, w_out, g, b, tm):
    n = x.shape[0]
    row = lambda width: pl.BlockSpec((tm, width), lambda i: (i, 0))
    vec = _const_spec((1, D_MODEL))
    return pl.pallas_call(
        _mix_ffn_kernel,
        grid=(n // tm,),
        in_specs=[row(D_MODEL), row(DV), row(SQ), _const_spec((DV + SQ, D_MODEL)), vec, vec,
                  _const_spec((D_MODEL, 2 * D_FF)), _const_spec((D_FF, D_MODEL)), vec, vec],
        out_specs=row(D_MODEL),
        out_shape=jax.ShapeDtypeStruct((n, D_MODEL), F32),
        scratch_shapes=[pltpu.VMEM((tm, D_FF), BF16)],
        compiler_params=pltpu.CompilerParams(
            dimension_semantics=("parallel",), vmem_limit_bytes=VMEM_LIMIT),
        name="mix_ffn_ln",
    )(x, yd, ys, w_mix, g_mix, b_mix, w_in, w_out, g, b)


def _proj_kernel(x_ref, w_ref, cos_ref, sin_ref,
                 qd_ref, kd_ref, vd_ref, qs_ref, ks_ref, vs_ref, *, tm):
    xb = x_ref[...].astype(BF16)
    cos = cos_ref[...]
    sin = sin_ref[...]

    def rope(t, scale):
        r = t * cos + pltpu.roll(t, LANES // 2, 1) * sin
        return r * scale if scale != 1.0 else r

    def put_transposed(ref, col, t):
        for blk in range(tm // BLOCK):
            ref[0, blk, col:col + LANES, :] = t[blk * BLOCK:(blk + 1) * BLOCK, :].T.astype(BF16)

    def put_rows(ref, col, t):
        ref[:, col:col + LANES] = t.astype(BF16)

    plan = []
    for j in range(DQ // LANES):
        plan.append((j * LANES, qd_ref, j * LANES, put_transposed, Q_SCALE))
    for j in range(DK // LANES):
        plan.append((DQ + j * LANES, kd_ref, j * LANES, put_rows, 1.0))
    for j in range(SQ // LANES):
        plan.append((DQ + DK + j * LANES, qs_ref, j * LANES, put_transposed, Q_SCALE))
    plan.append((DQ + DK + SQ, ks_ref, 0, put_rows, 1.0))
    for j in range(DV // LANES):
        plan.append((ROPE_COLS + j * LANES, vd_ref, j * LANES, put_transposed, None))
    plan.append((ROPE_COLS + DV, vs_ref, 0, put_transposed, None))

    for first in range(0, len(plan), 2):
        lo = plan[first][0]
        t2 = _dot(xb, w_ref[:, lo:lo + 2 * LANES])
        for half, (_, ref, col, put, scale) in enumerate(plan[first:first + 2]):
            t = t2[:, half * LANES:(half + 1) * LANES]
            put(ref, col, t if scale is None else rope(t, scale))


def _in_proj(x, w, cos, sin, batch, seq, tm):
    n = x.shape[0]
    nq = seq // BLOCK
    tpb = seq // tm
    sub = tm // BLOCK
    const = _const_spec
    row = lambda width: pl.BlockSpec((tm, width), lambda i: (i, 0))
    tblk = lambda rows: pl.BlockSpec((1, sub, rows, BLOCK), lambda i: (i // tpb, i % tpb, 0, 0))
    tshape = lambda rows: jax.ShapeDtypeStruct((batch, nq, rows, BLOCK), BF16)
    return pl.pallas_call(
        functools.partial(_proj_kernel, tm=tm),
        grid=(n // tm,),
        in_specs=[row(D_MODEL), const((D_MODEL, IN_COLS)), row(LANES), row(LANES)],
        out_specs=[tblk(DQ), row(DK), tblk(DV), tblk(SQ), row(SK), tblk(SV)],
        out_shape=[tshape(DQ), jax.ShapeDtypeStruct((n, DK), BF16), tshape(DV),
                   tshape(SQ), jax.ShapeDtypeStruct((n, SK), BF16), tshape(SV)],
        compiler_params=pltpu.CompilerParams(
            dimension_semantics=("parallel",), vmem_limit_bytes=VMEM_LIMIT),
        name="in_proj_rope",
    )(x, w, cos, sin)


def _head_select_mask():
    r = lax.broadcasted_iota(jnp.int32, (LANES, 1), 0)
    return (r % HEAD_DIM) < HALF


def _pair_queries(qt, sel0):
    zero = jnp.zeros_like(qt)
    return jnp.concatenate([jnp.where(sel0, qt, zero), jnp.where(sel0, zero, qt)], axis=1)


DIFF_KV_TILE = 256
DIFF_SCORE_BUFFERS = 4


def _diff_attn_kernel(q_ref, k_ref, v_ref, lam_ref, prm_ref, y_ref, *s_refs, seq):
    lv = lam_ref[...]
    a1 = jnp.sum(lv[0:1, :] * lv[1:2, :], axis=-1, keepdims=True)
    a2 = jnp.sum(lv[2:3, :] * lv[3:4, :], axis=-1, keepdims=True)
    lam = jnp.exp(a1) - jnp.exp(a2) + prm_ref[2:3, 0:1]
    gain = prm_ref[0:1, :]
    one_minus_init = prm_ref[1:2, :]
    sel0 = _head_select_mask()
    tk = min(DIFF_KV_TILE, seq)
    nq = seq // BLOCK
    tiles = [(j * tk, (j + 1) * tk) for j in range(seq // tk)]

    def scores(i, s_ref):
        qm = _pair_queries(q_ref[0, i], sel0)
        m = None
        for lo, hi in tiles:
            s = _dot(k_ref[lo:hi, :], qm)
            s_ref[lo:hi, :] = s
            tile_max = jnp.max(s, axis=0, keepdims=True)
            m = tile_max if m is None else jnp.maximum(m, tile_max)
        return m

    def finish(i, s_ref, m):
        l = jnp.zeros((1, 2 * BLOCK), F32)
        o = jnp.zeros((LANES, 2 * BLOCK), F32)
        for lo, hi in tiles:
            p = jnp.exp2(s_ref[lo:hi, :] - m)
            l = l + jnp.sum(p, axis=0, keepdims=True)
            vt = jnp.concatenate([v_ref[0, t] for t in range(lo // BLOCK, hi // BLOCK)], axis=1)
            o = o + _dot(vt, p.astype(BF16))
        on = o * (1.0 / l)
        d = (on[:, :BLOCK] - lam * on[:, BLOCK:]).T
        ms = jnp.mean(d * d, axis=-1, keepdims=True)
        y = d * lax.rsqrt(ms + RMS_EPS) * gain * one_minus_init
        y_ref[pl.ds(pl.multiple_of(i * BLOCK, BLOCK), BLOCK), :] = y.astype(BF16)

    nbuf = len(s_refs)

    def block_group(t, m):
        for k in range(nbuf):
            i = nbuf * t + k
            m_next = scores(jnp.minimum(i + 1, nq - 1), s_refs[(k + 1) % nbuf])
            finish(i, s_refs[k], m)
            m = m_next
        return m

    lax.fori_loop(0, nq // nbuf, block_group, scores(0, s_refs[0]))


def _diff_attn(qd, kd, vd, lam_vec, prm, batch, seq):
    nq = seq // BLOCK
    tspec = pl.BlockSpec((1, nq, LANES, BLOCK), lambda b, h: (b, 0, h, 0))
    kspec = pl.BlockSpec((seq, LANES), lambda b, h: (b, h))
    small = lambda shape: pl.BlockSpec(shape, lambda b, h: (0, 0))
    return pl.pallas_call(
        functools.partial(_diff_attn_kernel, seq=seq),
        grid=(batch, DIFF_HEADS),
        in_specs=[tspec, kspec, tspec, small((4, HEAD_DIM)), small((8, LANES))],
        out_specs=pl.BlockSpec((seq, LANES), lambda b, h: (b, h)),
        out_shape=jax.ShapeDtypeStruct((batch * seq, DV), BF16),
        scratch_shapes=[pltpu.VMEM((seq, 2 * BLOCK), F32)] * DIFF_SCORE_BUFFERS,
        compiler_params=pltpu.CompilerParams(
            dimension_semantics=("parallel", "parallel"), vmem_limit_bytes=VMEM_LIMIT),
        name="diff_attn",
    )(qd, kd, vd, lam_vec, prm)


SWA_BLOCKS_PER_STEP = 4


def _swa_kernel(q_ref, k_ref, v_ref, sink_ref, y_ref, *, seq):
    nq = seq // BLOCK
    sel0 = _head_select_mask()
    r = lax.broadcasted_iota(jnp.int32, (BLOCK, BLOCK), 0)
    c = lax.broadcasted_iota(jnp.int32, (BLOCK, BLOCK), 1)
    prev_band = jnp.where(c <= r, 0.0, NEG).astype(F32)
    next_band = jnp.where(r <= c, 0.0, NEG).astype(F32)

    rows = lambda blk: pl.ds(pl.multiple_of(blk * BLOCK, BLOCK), BLOCK)
    unroll = SWA_BLOCKS_PER_STEP if nq % SWA_BLOCKS_PER_STEP == 0 else 1

    def load_block(n):
        n_prev = jnp.maximum(n - 1, 0)
        n_next = jnp.minimum(n + 1, nq - 1)
        edge_prev = jnp.where(n >= 1, 0.0, NEG).astype(F32)
        edge_next = jnp.where(n <= nq - 2, 0.0, NEG).astype(F32)
        bias_prev = jnp.concatenate([prev_band + edge_prev] * 2, axis=1)
        bias_next = jnp.concatenate([next_band + edge_next] * 2, axis=1)
        bias = (bias_prev, bias_next)
        kb = jnp.concatenate([k_ref[rows(n_prev), :], k_ref[rows(n), :], k_ref[rows(n_next), :]],
                             axis=0)
        vb = jnp.concatenate([v_ref[0, n_prev], v_ref[0, n], v_ref[0, n_next]], axis=1)
        qms = [_pair_queries(q_ref[0, n, g * LANES:(g + 1) * LANES, :], sel0)
               for g in range(SWA_GROUP)]
        return kb, vb, bias, qms

    def step(t, carry):
        blocks = [unroll * t + u for u in range(unroll)]
        loaded = [load_block(n) for n in blocks]
        def masked_scores(kb, qm, bias):
            s = _dot(kb, qm)
            return jnp.concatenate([s[:BLOCK] + bias[0], s[BLOCK:2 * BLOCK],
                                    s[2 * BLOCK:] + bias[1]], axis=0)

        scores = [[masked_scores(kb, qm, bias) for qm in qms] for kb, _, bias, qms in loaded]
        for n, (_, vb, _, _), block_scores in zip(blocks, loaded, scores):
            outs = []
            for g, s in enumerate(block_scores):
                sink = sink_ref[:, g * 2 * BLOCK:(g + 1) * 2 * BLOCK]
                m = jnp.maximum(jnp.max(s, axis=0, keepdims=True), sink)
                e = jnp.exp2(s - m)
                den = jnp.sum(e, axis=0, keepdims=True) + jnp.exp2(sink - m)
                on = _dot(vb, e.astype(BF16)) * (1.0 / den)
                yt = jnp.concatenate([on[:HEAD_DIM, :BLOCK], on[HEAD_DIM:, BLOCK:]], axis=0)
                outs.append(yt.T.astype(BF16))
            y_ref[rows(n), :] = jnp.concatenate(outs, axis=1)
        return carry

    lax.fori_loop(0, nq // unroll, step, 0)


def _swa_attn(qs, ks, vs, sink_vec, batch, seq):
    nq = seq // BLOCK
    return pl.pallas_call(
        functools.partial(_swa_kernel, seq=seq),
        grid=(batch,),
        in_specs=[pl.BlockSpec((1, nq, SQ, BLOCK), lambda b: (b, 0, 0, 0)),
                  pl.BlockSpec((seq, SK), lambda b: (b, 0)),
                  pl.BlockSpec((1, nq, SV, BLOCK), lambda b: (b, 0, 0, 0)),
                  pl.BlockSpec((1, SWA_Q_HEADS * BLOCK), lambda b: (0, 0))],
        out_specs=pl.BlockSpec((seq, SQ), lambda b: (b, 0)),
        out_shape=jax.ShapeDtypeStruct((batch * seq, SQ), BF16),
        compiler_params=pltpu.CompilerParams(
            dimension_semantics=("parallel",), vmem_limit_bytes=VMEM_LIMIT),
        name="swa_attn",
    )(qs, ks, vs, sink_vec)


def _token_tile(n, want):
    return want if n % want == 0 else BLOCK


def kernel(x, positions, w_in, w_out, diff_lambda, diff_subln_g, swa_sink,
           ffn1_w_in, ffn1_w_out, ffn2_w_in, ffn2_w_out, ln_g, ln_b):
    batch, seq, _ = x.shape
    n = batch * seq
    tm = _token_tile(seq, 512)
    h = x.reshape(n, D_MODEL)
    cos, sin = _rope_tables(positions)

    w_in_p = jnp.take(w_in, jnp.asarray(_IN_PERM), axis=2).astype(BF16)
    w_out_p = jnp.take(w_out, jnp.asarray(_OUT_PERM), axis=1).astype(BF16)
    sink = swa_sink.reshape(DEPTH, SWA_KV_HEADS, SWA_GROUP).transpose(0, 2, 1) * LOG2E
    sink_vec = jnp.repeat(sink.reshape(DEPTH, 1, SWA_Q_HEADS), BLOCK, axis=2)

    for l in range(DEPTH):
        lambda_init = 0.8 - 0.6 * math.exp(-0.3 * l)
        g = lambda i: ln_g[l, i].reshape(1, D_MODEL)
        b = lambda i: ln_b[l, i].reshape(1, D_MODEL)
        prm = jnp.zeros((8, LANES), F32)
        prm = prm.at[0].set(diff_subln_g[l]).at[1].set(1.0 - lambda_init).at[2].set(lambda_init)

        h = _ffn_ln(h, ffn1_w_in[l].astype(BF16), ffn1_w_out[l].astype(BF16), g(0), b(0), tm)
        qd, kd, vd, qs, ks, vs = _in_proj(h, w_in_p[l], cos, sin, batch, seq, tm)
        yd = _diff_attn(qd, kd, vd, diff_lambda[l], prm, batch, seq)
        ys = _swa_attn(qs, ks, vs, sink_vec[l], batch, seq)
        h = _mix_ffn_ln(h, yd, ys, w_out_p[l], g(1), b(1),
                        ffn2_w_in[l].astype(BF16), ffn2_w_out[l].astype(BF16), g(2), b(2), tm)
    return h.reshape(batch, seq, D_MODEL)
```

```python
import functools
import math

import jax
import jax.numpy as jnp
from jax import lax
from jax.experimental import pallas as pl
from jax.experimental.pallas import tpu as pltpu

D_MODEL = 1024
DEPTH = 4
HEAD_DIM = 64
HALF = HEAD_DIM // 2
DIFF_HEADS = 4
SWA_Q_HEADS = 8
SWA_KV_HEADS = 2
SWA_GROUP = SWA_Q_HEADS // SWA_KV_HEADS
BLOCK = 128
D_FF = 2816
ROPE_THETA = 10000.0
ALPHA = (2.0 * DEPTH) ** 0.25
LN_EPS = 1e-5
RMS_EPS = 1e-5
LOG2E = 1.4426950408889634
Q_SCALE = (HEAD_DIM ** -0.5) * LOG2E
NEG = -1e30

LANES = 128
DQ = DK = DV = 512
SQ = 512
SK = SV = 128
IN_COLS = DQ + DK + DV + SQ + SK + SV
ROPE_COLS = DQ + DK + SQ + SK

VMEM_LIMIT = 56 * 1024 * 1024

BF16 = jnp.bfloat16
F32 = jnp.float32


def _permute_in_proj(w):
    lead = w.shape[:-1]
    dq, dk, dv = w[..., :DQ], w[..., DQ:DQ + DK], w[..., DQ + DK:DQ + DK + DV]
    sq0 = DQ + DK + DV
    sq, sk, sv = w[..., sq0:sq0 + SQ], w[..., sq0 + SQ:sq0 + SQ + SK], w[..., sq0 + SQ + SK:]
    pair = lambda t: t.reshape(lead + (DIFF_HEADS, 2, 2, HALF)).swapaxes(-3, -2).reshape(lead + (DQ,))
    sq = jnp.moveaxis(sq.reshape(lead + (SWA_KV_HEADS, SWA_GROUP, 2, HALF)), -4, -2)
    sk = sk.reshape(lead + (SWA_KV_HEADS, 2, HALF)).swapaxes(-3, -2)
    return jnp.concatenate(
        [pair(dq), pair(dk), sq.reshape(lead + (SQ,)), sk.reshape(lead + (SK,)), dv, sv], axis=-1)


def _permute_out_proj(w):
    lead, tail = w.shape[:-2], w.shape[-1:]
    swa = w[..., DV:, :].reshape(lead + (SWA_KV_HEADS, SWA_GROUP, HEAD_DIM) + tail)
    swa = swa.swapaxes(-4, -3).reshape(lead + (SQ,) + tail)
    return jnp.concatenate([w[..., :DV, :], swa], axis=-2)


def _layer_norm(z, g, b):
    mu = jnp.mean(z, axis=-1, keepdims=True)
    zc = z - mu
    var = jnp.mean(zc * zc, axis=-1, keepdims=True)
    return zc * lax.rsqrt(var + LN_EPS) * g + b


def _dot(a, b):
    return jnp.dot(a, b, preferred_element_type=F32)


def _rope_table_kernel(pos_ref, inv_ref, sign_ref, cos_ref, sin_ref):
    ang = pos_ref[...] * inv_ref[...]
    cos_ref[...] = jnp.cos(ang)
    sin_ref[...] = jnp.sin(ang) * sign_ref[...]


def _rope_tables(positions):
    n = positions.size
    inv = jnp.power(ROPE_THETA, -jnp.arange(0, HEAD_DIM, 2, dtype=F32) / HEAD_DIM)
    inv128 = jnp.tile(inv, LANES // HALF).reshape(1, LANES)
    sign = jnp.where(jnp.arange(LANES) < LANES // 2, -1.0, 1.0).astype(F32).reshape(1, LANES)
    pos = jnp.broadcast_to(positions.reshape(n, 1).astype(F32), (n, LANES))
    tm = min(n, 2048)
    row = pl.BlockSpec((tm, LANES), lambda i: (i, 0))
    vec = pl.BlockSpec((1, LANES), lambda i: (0, 0))
    return pl.pallas_call(
        _rope_table_kernel,
        grid=(n // tm,),
        in_specs=[row, vec, vec],
        out_specs=[row, row],
        out_shape=[jax.ShapeDtypeStruct((n, LANES), F32)] * 2,
        name="rope_tables",
    )(pos, inv128, sign)


FFN_CHUNK = 256


FFN_OUT_SPLITS = 2


def _ffn_residual_ln(x, wi_ref, wo_ref, g_ref, b_ref, o_ref, act_ref):
    xb = x.astype(BF16)
    for c in range(D_FF // FFN_CHUNK):
        lo = c * FFN_CHUNK
        gate = _dot(xb, wi_ref[:, lo:lo + FFN_CHUNK])
        up = _dot(xb, wi_ref[:, D_FF + lo:D_FF + lo + FFN_CHUNK])
        act = gate * (1.0 / (1.0 + jnp.exp(-gate))) * up
        act_ref[:, lo:lo + FFN_CHUNK] = act.astype(BF16)
    rows = x.shape[0] // FFN_OUT_SPLITS
    for r in range(FFN_OUT_SPLITS):
        sl = slice(r * rows, (r + 1) * rows)
        y = _dot(act_ref[sl, :], wo_ref[...])
        o_ref[sl, :] = _layer_norm(ALPHA * x[sl, :] + 0.5 * y, g_ref[...], b_ref[...])


def _ffn_kernel(x_ref, wi_ref, wo_ref, g_ref, b_ref, o_ref, act_ref):
    _ffn_residual_ln(x_ref[...], wi_ref, wo_ref, g_ref, b_ref, o_ref, act_ref)


def _const_spec(shape):
    return pl.BlockSpec(shape, lambda i: (0, 0), pipeline_mode=pl.Buffered(1))


def _layer_spec(shape, layer):
    return pl.BlockSpec((None,) + shape, lambda i: (layer, 0, 0), pipeline_mode=pl.Buffered(1))


def _ffn_ln(x, w_in, w_out, g, b, layer, tm):
    n = x.shape[0]
    row = pl.BlockSpec((tm, D_MODEL), lambda i: (i, 0))
    return pl.pallas_call(
        _ffn_kernel,
        grid=(n // tm,),
        in_specs=[row, _layer_spec((D_MODEL, 2 * D_FF), layer), _layer_spec((D_FF, D_MODEL), layer),
                  _const_spec((1, D_MODEL)), _const_spec((1, D_MODEL))],
        out_specs=row,
        out_shape=jax.ShapeDtypeStruct((n, D_MODEL), F32),
        scratch_shapes=[pltpu.VMEM((tm, D_FF), BF16)],
        compiler_params=pltpu.CompilerParams(
            dimension_semantics=("parallel",), vmem_limit_bytes=VMEM_LIMIT),
        name="ffn_ln",
    )(x, w_in, w_out, g, b)


def _mix_ffn_kernel(x_ref, yd_ref, ys_ref, wm_ref, gm_ref, bm_ref,
                    wi_ref, wo_ref, g_ref, b_ref, o_ref, act_ref):
    mix = _dot(yd_ref[...], wm_ref[:DV, :]) + _dot(ys_ref[...], wm_ref[DV:, :])
    x = _layer_norm(ALPHA * x_ref[...] + mix, gm_ref[...], bm_ref[...])
    _ffn_residual_ln(x, wi_ref, wo_ref, g_ref, b_ref, o_ref, act_ref)


def _mix_ffn_ln(x, yd, ys, w_mix, g_mix, b_mix, w_in, w_out, g, b, layer, tm):
    n = x.shape[0]
    row = lambda width: pl.BlockSpec((tm, width), lambda i: (i, 0))
    vec = _const_spec((1, D_MODEL))
    return pl.pallas_call(
        _mix_ffn_kernel,
        grid=(n // tm,),
        in_specs=[row(D_MODEL), row(DV), row(SQ), _layer_spec((DV + SQ, D_MODEL), layer), vec, vec,
                  _layer_spec((D_MODEL, 2 * D_FF), layer), _layer_spec((D_FF, D_MODEL), layer),
                  vec, vec],
        out_specs=row(D_MODEL),
        out_shape=jax.ShapeDtypeStruct((n, D_MODEL), F32),
        scratch_shapes=[pltpu.VMEM((tm, D_FF), BF16)],
        compiler_params=pltpu.CompilerParams(
            dimension_semantics=("parallel",), vmem_limit_bytes=VMEM_LIMIT),
        name="mix_ffn_ln",
    )(x, yd, ys, w_mix, g_mix, b_mix, w_in, w_out, g, b)


def _proj_kernel(x_ref, w_ref, cos_ref, sin_ref,
                 qd_ref, kd_ref, vd_ref, qs_ref, ks_ref, vs_ref, *, tm):
    xb = x_ref[...].astype(BF16)
    cos = cos_ref[...]
    sin = sin_ref[...]

    def rope(t, scale):
        r = t * cos + pltpu.roll(t, LANES // 2, 1) * sin
        return r * scale if scale != 1.0 else r

    def put_transposed(ref, col, t):
        for blk in range(tm // BLOCK):
            ref[0, blk, col:col + LANES, :] = t[blk * BLOCK:(blk + 1) * BLOCK, :].T.astype(BF16)

    def put_rows(ref, col, t):
        ref[:, col:col + LANES] = t.astype(BF16)

    plan = []
    for j in range(DQ // LANES):
        plan.append((j * LANES, qd_ref, j * LANES, put_transposed, Q_SCALE))
    for j in range(DK // LANES):
        plan.append((DQ + j * LANES, kd_ref, j * LANES, put_rows, 1.0))
    for j in range(SQ // LANES):
        plan.append((DQ + DK + j * LANES, qs_ref, j * LANES, put_transposed, Q_SCALE))
    plan.append((DQ + DK + SQ, ks_ref, 0, put_rows, 1.0))
    for j in range(DV // LANES):
        plan.append((ROPE_COLS + j * LANES, vd_ref, j * LANES, put_transposed, None))
    plan.append((ROPE_COLS + DV, vs_ref, 0, put_transposed, None))

    for first in range(0, len(plan), 2):
        lo = plan[first][0]
        t2 = _dot(xb, w_ref[:, lo:lo + 2 * LANES])
        for half, (_, ref, col, put, scale) in enumerate(plan[first:first + 2]):
            t = t2[:, half * LANES:(half + 1) * LANES]
            put(ref, col, t if scale is None else rope(t, scale))


def _in_proj(x, w, cos, sin, layer, batch, seq, tm):
    n = x.shape[0]
    nq = seq // BLOCK
    tpb = seq // tm
    sub = tm // BLOCK
    row = lambda width: pl.BlockSpec((tm, width), lambda i: (i, 0))
    tblk = lambda rows: pl.BlockSpec((1, sub, rows, BLOCK), lambda i: (i // tpb, i % tpb, 0, 0))
    tshape = lambda rows: jax.ShapeDtypeStruct((batch, nq, rows, BLOCK), BF16)
    return pl.pallas_call(
        functools.partial(_proj_kernel, tm=tm),
        grid=(n // tm,),
        in_specs=[row(D_MODEL), _layer_spec((D_MODEL, IN_COLS), layer), row(LANES), row(LANES)],
        out_specs=[tblk(DQ), row(DK), tblk(DV), tblk(SQ), row(SK), tblk(SV)],
        out_shape=[tshape(DQ), jax.ShapeDtypeStruct((n, DK), BF16), tshape(DV),
                   tshape(SQ), jax.ShapeDtypeStruct((n, SK), BF16), tshape(SV)],
        compiler_params=pltpu.CompilerParams(
            dimension_semantics=("parallel",), vmem_limit_bytes=VMEM_LIMIT),
        name="in_proj_rope",
    )(x, w, cos, sin)


def _head_select_mask():
    r = lax.broadcasted_iota(jnp.int32, (LANES, 1), 0)
    return (r % HEAD_DIM) < HALF


def _pair_queries(qt, sel0):
    zero = jnp.zeros_like(qt)
    return jnp.concatenate([jnp.where(sel0, qt, zero), jnp.where(sel0, zero, qt)], axis=1)


DIFF_KV_TILE = 256
DIFF_SCORE_BUFFERS = 4
DIFF_BLOCKS_PER_STEP = 8


def _diff_attn_kernel(q_ref, k_ref, v_ref, lam_ref, prm_ref, y_ref, *s_refs, seq):
    lv = lam_ref[...]
    a1 = jnp.sum(lv[0:1, :] * lv[1:2, :], axis=-1, keepdims=True)
    a2 = jnp.sum(lv[2:3, :] * lv[3:4, :], axis=-1, keepdims=True)
    lam = jnp.exp(a1) - jnp.exp(a2) + prm_ref[2:3, 0:1]
    gain = prm_ref[0:1, :]
    one_minus_init = prm_ref[1:2, :]
    sel0 = _head_select_mask()
    tk = min(DIFF_KV_TILE, seq)
    nq = seq // BLOCK
    tiles = [(j * tk, (j + 1) * tk) for j in range(seq // tk)]

    def scores(i, s_ref):
        qm = _pair_queries(q_ref[0, i], sel0)
        m = None
        for lo, hi in tiles:
            s = _dot(k_ref[lo:hi, :], qm)
            s_ref[lo:hi, :] = s
            tile_max = jnp.max(s, axis=0, keepdims=True)
            m = tile_max if m is None else jnp.maximum(m, tile_max)
        return m

    def finish(i, s_ref, m):
        l = jnp.zeros((1, 2 * BLOCK), F32)
        o = jnp.zeros((LANES, 2 * BLOCK), F32)
        for lo, hi in tiles:
            p = jnp.exp2(s_ref[lo:hi, :] - m)
            l = l + jnp.sum(p, axis=0, keepdims=True)
            vt = jnp.concatenate([v_ref[0, t] for t in range(lo // BLOCK, hi // BLOCK)], axis=1)
            o = o + _dot(vt, p.astype(BF16))
        on = o * (1.0 / l)
        d = (on[:, :BLOCK] - lam * on[:, BLOCK:]).T
        ms = jnp.mean(d * d, axis=-1, keepdims=True)
        y = d * lax.rsqrt(ms + RMS_EPS) * gain * one_minus_init
        y_ref[pl.ds(pl.multiple_of(i * BLOCK, BLOCK), BLOCK), :] = y.astype(BF16)

    nbuf = len(s_refs)
    group = DIFF_BLOCKS_PER_STEP if nq % DIFF_BLOCKS_PER_STEP == 0 else nbuf

    def block_group(t, m):
        for k in range(group):
            i = group * t + k
            m_next = scores(jnp.minimum(i + 1, nq - 1), s_refs[(k + 1) % nbuf])
            finish(i, s_refs[k % nbuf], m)
            m = m_next
        return m

    lax.fori_loop(0, nq // group, block_group, scores(0, s_refs[0]))


def _diff_attn(qd, kd, vd, lam_vec, prm, batch, seq):
    nq = seq // BLOCK
    tspec = pl.BlockSpec((1, nq, LANES, BLOCK), lambda b, h: (b, 0, h, 0))
    kspec = pl.BlockSpec((seq, LANES), lambda b, h: (b, h))
    small = lambda shape: pl.BlockSpec(shape, lambda b, h: (0, 0))
    return pl.pallas_call(
        functools.partial(_diff_attn_kernel, seq=seq),
        grid=(batch, DIFF_HEADS),
        in_specs=[tspec, kspec, tspec, small((4, HEAD_DIM)), small((8, LANES))],
        out_specs=pl.BlockSpec((seq, LANES), lambda b, h: (b, h)),
        out_shape=jax.ShapeDtypeStruct((batch * seq, DV), BF16),
        scratch_shapes=[pltpu.VMEM((seq, 2 * BLOCK), F32)] * DIFF_SCORE_BUFFERS,
        compiler_params=pltpu.CompilerParams(
            dimension_semantics=("parallel", "parallel"), vmem_limit_bytes=VMEM_LIMIT),
        name="diff_attn",
    )(qd, kd, vd, lam_vec, prm)


SWA_BLOCKS_PER_STEP = 4


def _swa_kernel(q_ref, k_ref, v_ref, sink_ref, y_ref, *, seq):
    nq = seq // BLOCK
    sel0 = _head_select_mask()
    r = lax.broadcasted_iota(jnp.int32, (BLOCK, BLOCK), 0)
    c = lax.broadcasted_iota(jnp.int32, (BLOCK, BLOCK), 1)
    prev_band = jnp.where(c <= r, 0.0, NEG).astype(F32)
    next_band = jnp.where(r <= c, 0.0, NEG).astype(F32)

    rows = lambda blk: pl.ds(pl.multiple_of(blk * BLOCK, BLOCK), BLOCK)
    unroll = SWA_BLOCKS_PER_STEP if nq % SWA_BLOCKS_PER_STEP == 0 else 1

    def load_block(n):
        n_prev = jnp.maximum(n - 1, 0)
        n_next = jnp.minimum(n + 1, nq - 1)
        edge_prev = jnp.where(n >= 1, 0.0, NEG).astype(F32)
        edge_next = jnp.where(n <= nq - 2, 0.0, NEG).astype(F32)
        bias_prev = jnp.concatenate([prev_band + edge_prev] * 2, axis=1)
        bias_next = jnp.concatenate([next_band + edge_next] * 2, axis=1)
        bias = (bias_prev, bias_next)
        kb = jnp.concatenate([k_ref[rows(n_prev), :], k_ref[rows(n), :], k_ref[rows(n_next), :]],
                             axis=0)
        vb = jnp.concatenate([v_ref[0, n_prev], v_ref[0, n], v_ref[0, n_next]], axis=1)
        qms = [_pair_queries(q_ref[0, n, g * LANES:(g + 1) * LANES, :], sel0)
               for g in range(SWA_GROUP)]
        return kb, vb, bias, qms

    def step(t, carry):
        blocks = [unroll * t + u for u in range(unroll)]
        loaded = [load_block(n) for n in blocks]
        def masked_scores(kb, qm, bias):
            s = _dot(kb, qm)
            return jnp.concatenate([s[:BLOCK] + bias[0], s[BLOCK:2 * BLOCK],
                                    s[2 * BLOCK:] + bias[1]], axis=0)

        scores = [[masked_scores(kb, qm, bias) for qm in qms] for kb, _, bias, qms in loaded]
        for n, (_, vb, _, _), block_scores in zip(blocks, loaded, scores):
            outs = []
            for g, s in enumerate(block_scores):
                sink = sink_ref[:, g * 2 * BLOCK:(g + 1) * 2 * BLOCK]
                m = jnp.maximum(jnp.max(s, axis=0, keepdims=True), sink)
                e = jnp.exp2(s - m)
                den = jnp.sum(e, axis=0, keepdims=True) + jnp.exp2(sink - m)
                on = _dot(vb, e.astype(BF16)) * (1.0 / den)
                yt = jnp.concatenate([on[:HEAD_DIM, :BLOCK], on[HEAD_DIM:, BLOCK:]], axis=0)
                outs.append(yt.T.astype(BF16))
            y_ref[rows(n), :] = jnp.concatenate(outs, axis=1)
        return carry

    lax.fori_loop(0, nq // unroll, step, 0)


def _swa_attn(qs, ks, vs, sink_vec, batch, seq):
    nq = seq // BLOCK
    return pl.pallas_call(
        functools.partial(_swa_kernel, seq=seq),
        grid=(batch,),
        in_specs=[pl.BlockSpec((1, nq, SQ, BLOCK), lambda b: (b, 0, 0, 0)),
                  pl.BlockSpec((seq, SK), lambda b: (b, 0)),
                  pl.BlockSpec((1, nq, SV, BLOCK), lambda b: (b, 0, 0, 0)),
                  pl.BlockSpec((1, SWA_Q_HEADS * BLOCK), lambda b: (0, 0))],
        out_specs=pl.BlockSpec((seq, SQ), lambda b: (b, 0)),
        out_shape=jax.ShapeDtypeStruct((batch * seq, SQ), BF16),
        compiler_params=pltpu.CompilerParams(
            dimension_semantics=("parallel",), vmem_limit_bytes=VMEM_LIMIT),
        name="swa_attn",
    )(qs, ks, vs, sink_vec)


def _token_tile(n, want):
    return want if n % want == 0 else BLOCK


def kernel(x, positions, w_in, w_out, diff_lambda, diff_subln_g, swa_sink,
           ffn1_w_in, ffn1_w_out, ffn2_w_in, ffn2_w_out, ln_g, ln_b):
    batch, seq, _ = x.shape
    n = batch * seq
    tm = _token_tile(seq, 512)
    h = x.reshape(n, D_MODEL)
    cos, sin = _rope_tables(positions)

    w_in_p = _permute_in_proj(w_in).astype(BF16)
    w_out_p = _permute_out_proj(w_out).astype(BF16)
    ffn1_in, ffn1_out = ffn1_w_in.astype(BF16), ffn1_w_out.astype(BF16)
    ffn2_in, ffn2_out = ffn2_w_in.astype(BF16), ffn2_w_out.astype(BF16)
    sink = swa_sink.reshape(DEPTH, SWA_KV_HEADS, SWA_GROUP).transpose(0, 2, 1) * LOG2E
    sink_vec = jnp.repeat(sink.reshape(DEPTH, 1, SWA_Q_HEADS), BLOCK, axis=2)

    for l in range(DEPTH):
        lambda_init = 0.8 - 0.6 * math.exp(-0.3 * l)
        g = lambda i: ln_g[l, i].reshape(1, D_MODEL)
        b = lambda i: ln_b[l, i].reshape(1, D_MODEL)
        prm = jnp.zeros((8, LANES), F32)
        prm = prm.at[0].set(diff_subln_g[l]).at[1].set(1.0 - lambda_init).at[2].set(lambda_init)

        h = _ffn_ln(h, ffn1_in, ffn1_out, g(0), b(0), l, tm)
        qd, kd, vd, qs, ks, vs = _in_proj(h, w_in_p, cos, sin, l, batch, seq, tm)
        yd = _diff_attn(qd, kd, vd, diff_lambda[l], prm, batch, seq)
        ys = _swa_attn(qs, ks, vs, sink_vec[l], batch, seq)
        h = _mix_ffn_ln(h, yd, ys, w_out_p, g(1), b(1), ffn2_in, ffn2_out, g(2), b(2), l, tm)
    return h.reshape(batch, seq, D_MODEL)
```

```python
import functools
import math

import jax
import jax.numpy as jnp
from jax import lax
from jax.experimental import pallas as pl
from jax.experimental.pallas import tpu as pltpu

D_MODEL = 1024
DEPTH = 4
HEAD_DIM = 64
HALF = HEAD_DIM // 2
DIFF_HEADS = 4
SWA_Q_HEADS = 8
SWA_KV_HEADS = 2
SWA_GROUP = SWA_Q_HEADS // SWA_KV_HEADS
BLOCK = 128
D_FF = 2816
ROPE_THETA = 10000.0
ALPHA = (2.0 * DEPTH) ** 0.25
LN_EPS = 1e-5
RMS_EPS = 1e-5
LOG2E = 1.4426950408889634
Q_SCALE = (HEAD_DIM ** -0.5) * LOG2E
NEG = -1e30

LANES = 128
DQ = DK = DV = 512
SQ = 512
SK = SV = 128
IN_COLS = DQ + DK + DV + SQ + SK + SV
ROPE_COLS = DQ + DK + SQ + SK

VMEM_LIMIT = 56 * 1024 * 1024

BF16 = jnp.bfloat16
F32 = jnp.float32


def _permute_in_proj(w):
    lead = w.shape[:-1]
    dq, dk, dv = w[..., :DQ], w[..., DQ:DQ + DK], w[..., DQ + DK:DQ + DK + DV]
    sq0 = DQ + DK + DV
    sq, sk, sv = w[..., sq0:sq0 + SQ], w[..., sq0 + SQ:sq0 + SQ + SK], w[..., sq0 + SQ + SK:]
    pair = lambda t: t.reshape(lead + (DIFF_HEADS, 2, 2, HALF)).swapaxes(-3, -2).reshape(lead + (DQ,))
    sq = jnp.moveaxis(sq.reshape(lead + (SWA_KV_HEADS, SWA_GROUP, 2, HALF)), -4, -2)
    sk = sk.reshape(lead + (SWA_KV_HEADS, 2, HALF)).swapaxes(-3, -2)
    return jnp.concatenate(
        [pair(dq), pair(dk), sq.reshape(lead + (SQ,)), sk.reshape(lead + (SK,)), dv, sv], axis=-1)


def _permute_out_proj(w):
    lead, tail = w.shape[:-2], w.shape[-1:]
    swa = w[..., DV:, :].reshape(lead + (SWA_KV_HEADS, SWA_GROUP, HEAD_DIM) + tail)
    swa = swa.swapaxes(-4, -3).reshape(lead + (SQ,) + tail)
    return jnp.concatenate([w[..., :DV, :], swa], axis=-2)


def _layer_norm(z, g, b):
    mu = jnp.mean(z, axis=-1, keepdims=True)
    zc = z - mu
    var = jnp.mean(zc * zc, axis=-1, keepdims=True)
    return zc * lax.rsqrt(var + LN_EPS) * g + b


def _dot(a, b):
    return jnp.dot(a, b, preferred_element_type=F32)


def _rope_table_kernel(pos_ref, inv_ref, sign_ref, cos_ref, sin_ref):
    ang = pos_ref[...] * inv_ref[...]
    cos_ref[...] = jnp.cos(ang)
    sin_ref[...] = jnp.sin(ang) * sign_ref[...]


def _rope_tables(positions):
    n = positions.size
    inv = jnp.power(ROPE_THETA, -jnp.arange(0, HEAD_DIM, 2, dtype=F32) / HEAD_DIM)
    inv128 = jnp.tile(inv, LANES // HALF).reshape(1, LANES)
    sign = jnp.where(jnp.arange(LANES) < LANES // 2, -1.0, 1.0).astype(F32).reshape(1, LANES)
    pos = jnp.broadcast_to(positions.reshape(n, 1).astype(F32), (n, LANES))
    tm = min(n, 2048)
    row = pl.BlockSpec((tm, LANES), lambda i: (i, 0))
    vec = pl.BlockSpec((1, LANES), lambda i: (0, 0))
    return pl.pallas_call(
        _rope_table_kernel,
        grid=(n // tm,),
        in_specs=[row, vec, vec],
        out_specs=[row, row],
        out_shape=[jax.ShapeDtypeStruct((n, LANES), F32)] * 2,
        name="rope_tables",
    )(pos, inv128, sign)


FFN_CHUNK = 256


FFN_TOKEN_TILE = 1024
FFN_OUT_SPLITS = 4


def _ffn_residual_ln(x, wi_ref, wo_ref, g_ref, b_ref, o_ref, act_ref):
    xb = x.astype(BF16)
    for c in range(D_FF // FFN_CHUNK):
        lo = c * FFN_CHUNK
        gate = _dot(xb, wi_ref[:, lo:lo + FFN_CHUNK])
        up = _dot(xb, wi_ref[:, D_FF + lo:D_FF + lo + FFN_CHUNK])
        act = gate * (1.0 / (1.0 + jnp.exp(-gate))) * up
        act_ref[:, lo:lo + FFN_CHUNK] = act.astype(BF16)
    rows = x.shape[0] // FFN_OUT_SPLITS
    for r in range(FFN_OUT_SPLITS):
        sl = slice(r * rows, (r + 1) * rows)
        y = _dot(act_ref[sl, :], wo_ref[...])
        o_ref[sl, :] = _layer_norm(ALPHA * x[sl, :] + 0.5 * y, g_ref[...], b_ref[...])


def _ffn_kernel(x_ref, wi_ref, wo_ref, g_ref, b_ref, o_ref, act_ref):
    _ffn_residual_ln(x_ref[...], wi_ref, wo_ref, g_ref, b_ref, o_ref, act_ref)


def _const_spec(shape):
    return pl.BlockSpec(shape, lambda i: (0, 0), pipeline_mode=pl.Buffered(1))


def _layer_spec(shape, layer):
    return pl.BlockSpec((None,) + shape, lambda i: (layer, 0, 0), pipeline_mode=pl.Buffered(1))


def _ffn_ln(x, w_in, w_out, g, b, layer, tm):
    n = x.shape[0]
    row = pl.BlockSpec((tm, D_MODEL), lambda i: (i, 0))
    return pl.pallas_call(
        _ffn_kernel,
        grid=(n // tm,),
        in_specs=[row, _layer_spec((D_MODEL, 2 * D_FF), layer), _layer_spec((D_FF, D_MODEL), layer),
                  _const_spec((1, D_MODEL)), _const_spec((1, D_MODEL))],
        out_specs=row,
        out_shape=jax.ShapeDtypeStruct((n, D_MODEL), F32),
        scratch_shapes=[pltpu.VMEM((tm, D_FF), BF16)],
        compiler_params=pltpu.CompilerParams(
            dimension_semantics=("parallel",), vmem_limit_bytes=VMEM_LIMIT),
        name="ffn_ln",
    )(x, w_in, w_out, g, b)


def _mix_ffn_kernel(x_ref, yd_ref, ys_ref, wm_ref, gm_ref, bm_ref,
                    wi_ref, wo_ref, g_ref, b_ref, o_ref, act_ref):
    mix = _dot(yd_ref[...], wm_ref[:DV, :]) + _dot(ys_ref[...], wm_ref[DV:, :])
    x = _layer_norm(ALPHA * x_ref[...] + mix, gm_ref[...], bm_ref[...])
    _ffn_residual_ln(x, wi_ref, wo_ref, g_ref, b_ref, o_ref, act_ref)


def _mix_ffn_ln(x, yd, ys, w_mix, g_mix, b_mix, w_in, w_out, g, b, layer, tm):
    n = x.shape[0]
    row = lambda width: pl.BlockSpec((tm, width), lambda i: (i, 0))
    vec = _const_spec((1, D_MODEL))
    return pl.pallas_call(
        _mix_ffn_kernel,
        grid=(n // tm,),
        in_specs=[row(D_MODEL), row(DV), row(SQ), _layer_spec((DV + SQ, D_MODEL), layer), vec, vec,
                  _layer_spec((D_MODEL, 2 * D_FF), layer), _layer_spec((D_FF, D_MODEL), layer),
                  vec, vec],
        out_specs=row(D_MODEL),
        out_shape=jax.ShapeDtypeStruct((n, D_MODEL), F32),
        scratch_shapes=[pltpu.VMEM((tm, D_FF), BF16)],
        compiler_params=pltpu.CompilerParams(
            dimension_semantics=("parallel",), vmem_limit_bytes=VMEM_LIMIT),
        name="mix_ffn_ln",
    )(x, yd, ys, w_mix, g_mix, b_mix, w_in, w_out, g, b)


def _proj_kernel(x_ref, w_ref, cos_ref, sin_ref,
                 qd_ref, kd_ref, vd_ref, qs_ref, ks_ref, vs_ref, *, tm):
    xb = x_ref[...].astype(BF16)
    cos = cos_ref[...]
    sin = sin_ref[...]

    def rope(t, scale):
        r = t * cos + pltpu.roll(t, LANES // 2, 1) * sin
        return r * scale if scale != 1.0 else r

    def put_transposed(ref, col, t):
        for blk in range(tm // BLOCK):
            ref[0, blk, col:col + LANES, :] = t[blk * BLOCK:(blk + 1) * BLOCK, :].T.astype(BF16)

    def put_rows(ref, col, t):
        ref[:, col:col + LANES] = t.astype(BF16)

    plan = []
    for j in range(DQ // LANES):
        plan.append((j * LANES, qd_ref, j * LANES, put_transposed, Q_SCALE))
    for j in range(DK // LANES):
        plan.append((DQ + j * LANES, kd_ref, j * LANES, put_rows, 1.0))
    for j in range(SQ // LANES):
        plan.append((DQ + DK + j * LANES, qs_ref, j * LANES, put_transposed, Q_SCALE))
    plan.append((DQ + DK + SQ, ks_ref, 0, put_rows, 1.0))
    for j in range(DV // LANES):
        plan.append((ROPE_COLS + j * LANES, vd_ref, j * LANES, put_transposed, None))
    plan.append((ROPE_COLS + DV, vs_ref, 0, put_transposed, None))

    for first in range(0, len(plan), 2):
        lo = plan[first][0]
        t2 = _dot(xb, w_ref[:, lo:lo + 2 * LANES])
        for half, (_, ref, col, put, scale) in enumerate(plan[first:first + 2]):
            t = t2[:, half * LANES:(half + 1) * LANES]
            put(ref, col, t if scale is None else rope(t, scale))


def _in_proj(x, w, cos, sin, layer, batch, seq, tm):
    n = x.shape[0]
    nq = seq // BLOCK
    tpb = seq // tm
    sub = tm // BLOCK
    row = lambda width: pl.BlockSpec((tm, width), lambda i: (i, 0))
    tblk = lambda rows: pl.BlockSpec((1, sub, rows, BLOCK), lambda i: (i // tpb, i % tpb, 0, 0))
    tshape = lambda rows: jax.ShapeDtypeStruct((batch, nq, rows, BLOCK), BF16)
    return pl.pallas_call(
        functools.partial(_proj_kernel, tm=tm),
        grid=(n // tm,),
        in_specs=[row(D_MODEL), _layer_spec((D_MODEL, IN_COLS), layer), row(LANES), row(LANES)],
        out_specs=[tblk(DQ), row(DK), tblk(DV), tblk(SQ), row(SK), tblk(SV)],
        out_shape=[tshape(DQ), jax.ShapeDtypeStruct((n, DK), BF16), tshape(DV),
                   tshape(SQ), jax.ShapeDtypeStruct((n, SK), BF16), tshape(SV)],
        compiler_params=pltpu.CompilerParams(
            dimension_semantics=("parallel",), vmem_limit_bytes=VMEM_LIMIT),
        name="in_proj_rope",
    )(x, w, cos, sin)


def _head_select_mask():
    r = lax.broadcasted_iota(jnp.int32, (LANES, 1), 0)
    return (r % HEAD_DIM) < HALF


def _pair_queries(qt, sel0):
    zero = jnp.zeros_like(qt)
    return jnp.concatenate([jnp.where(sel0, qt, zero), jnp.where(sel0, zero, qt)], axis=1)


DIFF_KV_TILE = 256
DIFF_SCORE_BUFFERS = 4
DIFF_BLOCKS_PER_STEP = 8


def _diff_attn_kernel(q_ref, k_ref, v_ref, lam_ref, prm_ref, y_ref, *s_refs, seq):
    lv = lam_ref[...]
    a1 = jnp.sum(lv[0:1, :] * lv[1:2, :], axis=-1, keepdims=True)
    a2 = jnp.sum(lv[2:3, :] * lv[3:4, :], axis=-1, keepdims=True)
    lam = jnp.exp(a1) - jnp.exp(a2) + prm_ref[2:3, 0:1]
    gain = prm_ref[0:1, :]
    one_minus_init = prm_ref[1:2, :]
    sel0 = _head_select_mask()
    tk = min(DIFF_KV_TILE, seq)
    nq = seq // BLOCK
    tiles = [(j * tk, (j + 1) * tk) for j in range(seq // tk)]

    def scores(i, s_ref):
        qm = _pair_queries(q_ref[0, i], sel0)
        m = None
        for lo, hi in tiles:
            s = _dot(k_ref[lo:hi, :], qm)
            s_ref[lo:hi, :] = s
            tile_max = jnp.max(s, axis=0, keepdims=True)
            m = tile_max if m is None else jnp.maximum(m, tile_max)
        return m

    def finish(i, s_ref, m):
        l = jnp.zeros((1, 2 * BLOCK), F32)
        o = jnp.zeros((LANES, 2 * BLOCK), F32)
        for lo, hi in tiles:
            p = jnp.exp2(s_ref[lo:hi, :] - m)
            l = l + jnp.sum(p, axis=0, keepdims=True)
            vt = jnp.concatenate([v_ref[0, t] for t in range(lo // BLOCK, hi // BLOCK)], axis=1)
            o = o + _dot(vt, p.astype(BF16))
        on = o * (1.0 / l)
        d = (on[:, :BLOCK] - lam * on[:, BLOCK:]).T
        ms = jnp.mean(d * d, axis=-1, keepdims=True)
        y = d * lax.rsqrt(ms + RMS_EPS) * gain * one_minus_init
        y_ref[pl.ds(pl.multiple_of(i * BLOCK, BLOCK), BLOCK), :] = y.astype(BF16)

    nbuf = len(s_refs)
    group = DIFF_BLOCKS_PER_STEP if nq % DIFF_BLOCKS_PER_STEP == 0 else nbuf

    def block_group(t, m):
        for k in range(group):
            i = group * t + k
            m_next = scores(jnp.minimum(i + 1, nq - 1), s_refs[(k + 1) % nbuf])
            finish(i, s_refs[k % nbuf], m)
            m = m_next
        return m

    lax.fori_loop(0, nq // group, block_group, scores(0, s_refs[0]))


def _diff_attn(qd, kd, vd, lam_vec, prm, batch, seq):
    nq = seq // BLOCK
    tspec = pl.BlockSpec((1, nq, LANES, BLOCK), lambda b, h: (b, 0, h, 0))
    kspec = pl.BlockSpec((seq, LANES), lambda b, h: (b, h))
    small = lambda shape: pl.BlockSpec(shape, lambda b, h: (0, 0))
    return pl.pallas_call(
        functools.partial(_diff_attn_kernel, seq=seq),
        grid=(batch, DIFF_HEADS),
        in_specs=[tspec, kspec, tspec, small((4, HEAD_DIM)), small((8, LANES))],
        out_specs=pl.BlockSpec((seq, LANES), lambda b, h: (b, h)),
        out_shape=jax.ShapeDtypeStruct((batch * seq, DV), BF16),
        scratch_shapes=[pltpu.VMEM((seq, 2 * BLOCK), F32)] * DIFF_SCORE_BUFFERS,
        compiler_params=pltpu.CompilerParams(
            dimension_semantics=("parallel", "parallel"), vmem_limit_bytes=VMEM_LIMIT),
        name="diff_attn",
    )(qd, kd, vd, lam_vec, prm)


SWA_BLOCKS_PER_STEP = 4


def _swa_kernel(q_ref, k_ref, v_ref, sink_ref, y_ref, *, seq):
    nq = seq // BLOCK
    sel0 = _head_select_mask()
    r = lax.broadcasted_iota(jnp.int32, (BLOCK, BLOCK), 0)
    c = lax.broadcasted_iota(jnp.int32, (BLOCK, BLOCK), 1)
    prev_band = jnp.where(c <= r, 0.0, NEG).astype(F32)
    next_band = jnp.where(r <= c, 0.0, NEG).astype(F32)

    rows = lambda blk: pl.ds(pl.multiple_of(blk * BLOCK, BLOCK), BLOCK)
    unroll = SWA_BLOCKS_PER_STEP if nq % SWA_BLOCKS_PER_STEP == 0 else 1

    def load_block(n):
        n_prev = jnp.maximum(n - 1, 0)
        n_next = jnp.minimum(n + 1, nq - 1)
        edge_prev = jnp.where(n >= 1, 0.0, NEG).astype(F32)
        edge_next = jnp.where(n <= nq - 2, 0.0, NEG).astype(F32)
        bias_prev = jnp.concatenate([prev_band + edge_prev] * 2, axis=1)
        bias_next = jnp.concatenate([next_band + edge_next] * 2, axis=1)
        bias = (bias_prev, bias_next)
        kb = jnp.concatenate([k_ref[rows(n_prev), :], k_ref[rows(n), :], k_ref[rows(n_next), :]],
                             axis=0)
        vb = jnp.concatenate([v_ref[0, n_prev], v_ref[0, n], v_ref[0, n_next]], axis=1)
        qms = [_pair_queries(q_ref[0, n, g * LANES:(g + 1) * LANES, :], sel0)
               for g in range(SWA_GROUP)]
        return kb, vb, bias, qms

    def step(t, carry):
        blocks = [unroll * t + u for u in range(unroll)]
        loaded = [load_block(n) for n in blocks]
        def masked_scores(kb, qm, bias):
            s = _dot(kb, qm)
            return jnp.concatenate([s[:BLOCK] + bias[0], s[BLOCK:2 * BLOCK],
                                    s[2 * BLOCK:] + bias[1]], axis=0)

        scores = [[masked_scores(kb, qm, bias) for qm in qms] for kb, _, bias, qms in loaded]
        for n, (_, vb, _, _), block_scores in zip(blocks, loaded, scores):
            outs = []
            for g, s in enumerate(block_scores):
                sink = sink_ref[:, g * 2 * BLOCK:(g + 1) * 2 * BLOCK]
                m = jnp.maximum(jnp.max(s, axis=0, keepdims=True), sink)
                e = jnp.exp2(s - m)
                den = jnp.sum(e, axis=0, keepdims=True) + jnp.exp2(sink - m)
                on = _dot(vb, e.astype(BF16)) * (1.0 / den)
                yt = jnp.concatenate([on[:HEAD_DIM, :BLOCK], on[HEAD_DIM:, BLOCK:]], axis=0)
                outs.append(yt.T.astype(BF16))
            y_ref[rows(n), :] = jnp.concatenate(outs, axis=1)
        return carry

    lax.fori_loop(0, nq // unroll, step, 0)


def _swa_attn(qs, ks, vs, sink_vec, batch, seq):
    nq = seq // BLOCK
    return pl.pallas_call(
        functools.partial(_swa_kernel, seq=seq),
        grid=(batch,),
        in_specs=[pl.BlockSpec((1, nq, SQ, BLOCK), lambda b: (b, 0, 0, 0)),
                  pl.BlockSpec((seq, SK), lambda b: (b, 0)),
                  pl.BlockSpec((1, nq, SV, BLOCK), lambda b: (b, 0, 0, 0)),
                  pl.BlockSpec((1, SWA_Q_HEADS * BLOCK), lambda b: (0, 0))],
        out_specs=pl.BlockSpec((seq, SQ), lambda b: (b, 0)),
        out_shape=jax.ShapeDtypeStruct((batch * seq, SQ), BF16),
        compiler_params=pltpu.CompilerParams(
            dimension_semantics=("parallel",), vmem_limit_bytes=VMEM_LIMIT),
        name="swa_attn",
    )(qs, ks, vs, sink_vec)


def _token_tile(n, want):
    return want if n % want == 0 else BLOCK


def kernel(x, positions, w_in, w_out, diff_lambda, diff_subln_g, swa_sink,
           ffn1_w_in, ffn1_w_out, ffn2_w_in, ffn2_w_out, ln_g, ln_b):
    batch, seq, _ = x.shape
    n = batch * seq
    tm = _token_tile(seq, 512)
    tm_ffn = _token_tile(n, FFN_TOKEN_TILE)
    h = x.reshape(n, D_MODEL)
    cos, sin = _rope_tables(positions)

    w_in_p = _permute_in_proj(w_in).astype(BF16)
    w_out_p = _permute_out_proj(w_out).astype(BF16)
    ffn1_in, ffn1_out = ffn1_w_in.astype(BF16), ffn1_w_out.astype(BF16)
    ffn2_in, ffn2_out = ffn2_w_in.astype(BF16), ffn2_w_out.astype(BF16)
    sink = swa_sink.reshape(DEPTH, SWA_KV_HEADS, SWA_GROUP).transpose(0, 2, 1) * LOG2E
    sink_vec = jnp.repeat(sink.reshape(DEPTH, 1, SWA_Q_HEADS), BLOCK, axis=2)

    for l in range(DEPTH):
        lambda_init = 0.8 - 0.6 * math.exp(-0.3 * l)
        g = lambda i: ln_g[l, i].reshape(1, D_MODEL)
        b = lambda i: ln_b[l, i].reshape(1, D_MODEL)
        prm = jnp.zeros((8, LANES), F32)
        prm = prm.at[0].set(diff_subln_g[l]).at[1].set(1.0 - lambda_init).at[2].set(lambda_init)

        h = _ffn_ln(h, ffn1_in, ffn1_out, g(0), b(0), l, tm_ffn)
        qd, kd, vd, qs, ks, vs = _in_proj(h, w_in_p, cos, sin, l, batch, seq, tm)
        yd = _diff_attn(qd, kd, vd, diff_lambda[l], prm, batch, seq)
        ys = _swa_attn(qs, ks, vs, sink_vec[l], batch, seq)
        h = _mix_ffn_ln(h, yd, ys, w_out_p, g(1), b(1), ffn2_in, ffn2_out, g(2), b(2), l, tm_ffn)
    return h.reshape(batch, seq, D_MODEL)
```

```python
import functools
import math

import jax
import jax.numpy as jnp
from jax import lax
from jax.experimental import pallas as pl
from jax.experimental.pallas import tpu as pltpu

D_MODEL = 1024
DEPTH = 4
HEAD_DIM = 64
HALF = HEAD_DIM // 2
DIFF_HEADS = 4
SWA_Q_HEADS = 8
SWA_KV_HEADS = 2
SWA_GROUP = SWA_Q_HEADS // SWA_KV_HEADS
BLOCK = 128
D_FF = 2816
ROPE_THETA = 10000.0
ALPHA = (2.0 * DEPTH) ** 0.25
LN_EPS = 1e-5
RMS_EPS = 1e-5
LOG2E = 1.4426950408889634
Q_SCALE = (HEAD_DIM ** -0.5) * LOG2E
NEG = -1e30

LANES = 128
DQ = DK = DV = 512
SQ = 512
SK = SV = 128
IN_COLS = DQ + DK + DV + SQ + SK + SV
ROPE_COLS = DQ + DK + SQ + SK

VMEM_LIMIT = 56 * 1024 * 1024

BF16 = jnp.bfloat16
F32 = jnp.float32


def _permute_in_proj(w):
    lead = w.shape[:-1]
    dq, dk, dv = w[..., :DQ], w[..., DQ:DQ + DK], w[..., DQ + DK:DQ + DK + DV]
    sq0 = DQ + DK + DV
    sq, sk, sv = w[..., sq0:sq0 + SQ], w[..., sq0 + SQ:sq0 + SQ + SK], w[..., sq0 + SQ + SK:]
    pair = lambda t: t.reshape(lead + (DIFF_HEADS, 2, 2, HALF)).swapaxes(-3, -2).reshape(lead + (DQ,))
    sq = jnp.moveaxis(sq.reshape(lead + (SWA_KV_HEADS, SWA_GROUP, 2, HALF)), -4, -2)
    sk = sk.reshape(lead + (SWA_KV_HEADS, 2, HALF)).swapaxes(-3, -2)
    return jnp.concatenate(
        [pair(dq), pair(dk), sq.reshape(lead + (SQ,)), sk.reshape(lead + (SK,)), dv, sv], axis=-1)


def _permute_out_proj(w):
    lead, tail = w.shape[:-2], w.shape[-1:]
    swa = w[..., DV:, :].reshape(lead + (SWA_KV_HEADS, SWA_GROUP, HEAD_DIM) + tail)
    swa = swa.swapaxes(-4, -3).reshape(lead + (SQ,) + tail)
    return jnp.concatenate([w[..., :DV, :], swa], axis=-2)


def _layer_norm(z, g, b):
    mu = jnp.mean(z, axis=-1, keepdims=True)
    zc = z - mu
    var = jnp.mean(zc * zc, axis=-1, keepdims=True)
    return zc * lax.rsqrt(var + LN_EPS) * g + b


def _dot(a, b):
    return jnp.dot(a, b, preferred_element_type=F32)


ROPE_PACK = LANES // HALF
ROPE_TILE = 2048


def _rope_table_kernel(pos_ref, inv_ref, sign_ref, cos_ref, sin_ref):
    rows = pos_ref.shape[0]
    ang = pos_ref[...] * inv_ref[...]
    cos = jnp.cos(ang)
    sin = jnp.sin(ang)
    first_group = lax.broadcasted_iota(jnp.int32, (rows, LANES), 1) < HALF

    def spread(t, j):
        x = t if j == 0 else pltpu.roll(t, LANES - HALF * j, 1)
        x = jnp.where(first_group, x, 0.0)
        x = x + pltpu.roll(x, HALF, 1)
        return x + pltpu.roll(x, 2 * HALF, 1)

    for j in range(ROPE_PACK):
        cos_ref[j * rows:(j + 1) * rows, :] = spread(cos, j)
        sin_ref[j * rows:(j + 1) * rows, :] = spread(sin, j) * sign_ref[...]


def _rope_tables(positions):
    n = positions.size
    tile = min(n, ROPE_TILE)
    rows = tile // ROPE_PACK
    inv = jnp.power(ROPE_THETA, -jnp.arange(0, HEAD_DIM, 2, dtype=F32) / HEAD_DIM)
    inv128 = jnp.tile(inv, ROPE_PACK).reshape(1, LANES)
    sign = jnp.where(jnp.arange(LANES) < LANES // 2, -1.0, 1.0).astype(F32).reshape(1, LANES)
    pos = positions.reshape(n // tile, ROPE_PACK, rows).astype(F32).transpose(0, 2, 1)
    pos = jnp.repeat(pos, HALF, axis=2).reshape(n // ROPE_PACK, LANES)
    packed = pl.BlockSpec((rows, LANES), lambda i: (i, 0))
    row = pl.BlockSpec((tile, LANES), lambda i: (i, 0))
    vec = pl.BlockSpec((1, LANES), lambda i: (0, 0))
    return pl.pallas_call(
        _rope_table_kernel,
        grid=(n // tile,),
        in_specs=[packed, vec, vec],
        out_specs=[row, row],
        out_shape=[jax.ShapeDtypeStruct((n, LANES), F32)] * 2,
        name="rope_tables",
    )(pos, inv128, sign)


FFN_CHUNK = 256


FFN_TOKEN_TILE = 1024
FFN_OUT_SPLITS = 4


def _ffn_residual_ln(x, wi_ref, wo_ref, g_ref, b_ref, o_ref, act_ref):
    xb = x.astype(BF16)
    for c in range(D_FF // FFN_CHUNK):
        lo = c * FFN_CHUNK
        gate = _dot(xb, wi_ref[:, lo:lo + FFN_CHUNK])
        up = _dot(xb, wi_ref[:, D_FF + lo:D_FF + lo + FFN_CHUNK])
        act = gate * (1.0 / (1.0 + jnp.exp(-gate))) * up
        act_ref[:, lo:lo + FFN_CHUNK] = act.astype(BF16)
    rows = x.shape[0] // FFN_OUT_SPLITS
    for r in range(FFN_OUT_SPLITS):
        sl = slice(r * rows, (r + 1) * rows)
        y = _dot(act_ref[sl, :], wo_ref[...])
        o_ref[sl, :] = _layer_norm(ALPHA * x[sl, :] + 0.5 * y, g_ref[...], b_ref[...])


def _ffn_kernel(x_ref, wi_ref, wo_ref, g_ref, b_ref, o_ref, act_ref):
    _ffn_residual_ln(x_ref[...], wi_ref, wo_ref, g_ref, b_ref, o_ref, act_ref)


def _const_spec(shape):
    return pl.BlockSpec(shape, lambda i: (0, 0), pipeline_mode=pl.Buffered(1))


def _layer_spec(shape, layer):
    return pl.BlockSpec((None,) + shape, lambda i: (layer, 0, 0), pipeline_mode=pl.Buffered(1))


def _ffn_ln(x, w_in, w_out, g, b, layer, tm):
    n = x.shape[0]
    row = pl.BlockSpec((tm, D_MODEL), lambda i: (i, 0))
    return pl.pallas_call(
        _ffn_kernel,
        grid=(n // tm,),
        in_specs=[row, _layer_spec((D_MODEL, 2 * D_FF), layer), _layer_spec((D_FF, D_MODEL), layer),
                  _const_spec((1, D_MODEL)), _const_spec((1, D_MODEL))],
        out_specs=row,
        out_shape=jax.ShapeDtypeStruct((n, D_MODEL), F32),
        scratch_shapes=[pltpu.VMEM((tm, D_FF), BF16)],
        compiler_params=pltpu.CompilerParams(
            dimension_semantics=("parallel",), vmem_limit_bytes=VMEM_LIMIT),
        name="ffn_ln",
    )(x, w_in, w_out, g, b)


def _mix_ffn_kernel(x_ref, yd_ref, ys_ref, wm_ref, gm_ref, bm_ref,
                    wi_ref, wo_ref, g_ref, b_ref, o_ref, act_ref):
    mix = _dot(yd_ref[...], wm_ref[:DV, :]) + _dot(ys_ref[...], wm_ref[DV:, :])
    x = _layer_norm(ALPHA * x_ref[...] + mix, gm_ref[...], bm_ref[...])
    _ffn_residual_ln(x, wi_ref, wo_ref, g_ref, b_ref, o_ref, act_ref)


def _mix_ffn_ln(x, yd, ys, w_mix, g_mix, b_mix, w_in, w_out, g, b, layer, tm):
    n = x.shape[0]
    row = lambda width: pl.BlockSpec((tm, width), lambda i: (i, 0))
    vec = _const_spec((1, D_MODEL))
    return pl.pallas_call(
        _mix_ffn_kernel,
        grid=(n // tm,),
        in_specs=[row(D_MODEL), row(DV), row(SQ), _layer_spec((DV + SQ, D_MODEL), layer), vec, vec,
                  _layer_spec((D_MODEL, 2 * D_FF), layer), _layer_spec((D_FF, D_MODEL), layer),
                  vec, vec],
        out_specs=row(D_MODEL),
        out_shape=jax.ShapeDtypeStruct((n, D_MODEL), F32),
        scratch_shapes=[pltpu.VMEM((tm, D_FF), BF16)],
        compiler_params=pltpu.CompilerParams(
            dimension_semantics=("parallel",), vmem_limit_bytes=VMEM_LIMIT),
        name="mix_ffn_ln",
    )(x, yd, ys, w_mix, g_mix, b_mix, w_in, w_out, g, b)


def _proj_kernel(x_ref, w_ref, cos_ref, sin_ref,
                 qd_ref, kd_ref, vd_ref, qs_ref, ks_ref, vs_ref, *, tm):
    xb = x_ref[...].astype(BF16)
    cos = cos_ref[...]
    sin = sin_ref[...]

    def rope(t, scale):
        r = t * cos + pltpu.roll(t, LANES // 2, 1) * sin
        return r * scale if scale != 1.0 else r

    def put_transposed(ref, col, t):
        for blk in range(tm // BLOCK):
            ref[0, blk, col:col + LANES, :] = t[blk * BLOCK:(blk + 1) * BLOCK, :].T.astype(BF16)

    def put_rows(ref, col, t):
        ref[:, col:col + LANES] = t.astype(BF16)

    plan = []
    for j in range(DQ // LANES):
        plan.append((j * LANES, qd_ref, j * LANES, put_transposed, Q_SCALE))
    for j in range(DK // LANES):
        plan.append((DQ + j * LANES, kd_ref, j * LANES, put_rows, 1.0))
    for j in range(SQ // LANES):
        plan.append((DQ + DK + j * LANES, qs_ref, j * LANES, put_transposed, Q_SCALE))
    plan.append((DQ + DK + SQ, ks_ref, 0, put_rows, 1.0))
    for j in range(DV // LANES):
        plan.append((ROPE_COLS + j * LANES, vd_ref, j * LANES, put_transposed, None))
    plan.append((ROPE_COLS + DV, vs_ref, 0, put_transposed, None))

    for first in range(0, len(plan), 2):
        lo = plan[first][0]
        t2 = _dot(xb, w_ref[:, lo:lo + 2 * LANES])
        for half, (_, ref, col, put, scale) in enumerate(plan[first:first + 2]):
            t = t2[:, half * LANES:(half + 1) * LANES]
            put(ref, col, t if scale is None else rope(t, scale))


def _in_proj(x, w, cos, sin, layer, batch, seq, tm):
    n = x.shape[0]
    nq = seq // BLOCK
    tpb = seq // tm
    sub = tm // BLOCK
    row = lambda width: pl.BlockSpec((tm, width), lambda i: (i, 0))
    tblk = lambda rows: pl.BlockSpec((1, sub, rows, BLOCK), lambda i: (i // tpb, i % tpb, 0, 0))
    tshape = lambda rows: jax.ShapeDtypeStruct((batch, nq, rows, BLOCK), BF16)
    return pl.pallas_call(
        functools.partial(_proj_kernel, tm=tm),
        grid=(n // tm,),
        in_specs=[row(D_MODEL), _layer_spec((D_MODEL, IN_COLS), layer), row(LANES), row(LANES)],
        out_specs=[tblk(DQ), row(DK), tblk(DV), tblk(SQ), row(SK), tblk(SV)],
        out_shape=[tshape(DQ), jax.ShapeDtypeStruct((n, DK), BF16), tshape(DV),
                   tshape(SQ), jax.ShapeDtypeStruct((n, SK), BF16), tshape(SV)],
        compiler_params=pltpu.CompilerParams(
            dimension_semantics=("parallel",), vmem_limit_bytes=VMEM_LIMIT),
        name="in_proj_rope",
    )(x, w, cos, sin)


def _head_select_mask():
    r = lax.broadcasted_iota(jnp.int32, (LANES, 1), 0)
    return (r % HEAD_DIM) < HALF


def _pair_queries(qt, sel0):
    zero = jnp.zeros_like(qt)
    return jnp.concatenate([jnp.where(sel0, qt, zero), jnp.where(sel0, zero, qt)], axis=1)


DIFF_KV_TILE = 256
DIFF_SCORE_BUFFERS = 4
DIFF_BLOCKS_PER_STEP = 8


def _diff_attn_kernel(q_ref, k_ref, v_ref, lam_ref, prm_ref, y_ref, *s_refs, seq):
    lv = lam_ref[...]
    a1 = jnp.sum(lv[0:1, :] * lv[1:2, :], axis=-1, keepdims=True)
    a2 = jnp.sum(lv[2:3, :] * lv[3:4, :], axis=-1, keepdims=True)
    lam = jnp.exp(a1) - jnp.exp(a2) + prm_ref[2:3, 0:1]
    gain = prm_ref[0:1, :]
    one_minus_init = prm_ref[1:2, :]
    sel0 = _head_select_mask()
    tk = min(DIFF_KV_TILE, seq)
    nq = seq // BLOCK
    tiles = [(j * tk, (j + 1) * tk) for j in range(seq // tk)]

    def scores(i, s_ref):
        qm = _pair_queries(q_ref[0, i], sel0)
        m = None
        for lo, hi in tiles:
            s = _dot(k_ref[lo:hi, :], qm)
            s_ref[lo:hi, :] = s
            tile_max = jnp.max(s, axis=0, keepdims=True)
            m = tile_max if m is None else jnp.maximum(m, tile_max)
        return m

    def finish(i, s_ref, m):
        l = jnp.zeros((1, 2 * BLOCK), F32)
        o = jnp.zeros((LANES, 2 * BLOCK), F32)
        for lo, hi in tiles:
            p = jnp.exp2(s_ref[lo:hi, :] - m)
            l = l + jnp.sum(p, axis=0, keepdims=True)
            vt = jnp.concatenate([v_ref[0, t] for t in range(lo // BLOCK, hi // BLOCK)], axis=1)
            o = o + _dot(vt, p.astype(BF16))
        on = o * (1.0 / l)
        d = (on[:, :BLOCK] - lam * on[:, BLOCK:]).T
        ms = jnp.mean(d * d, axis=-1, keepdims=True)
        y = d * lax.rsqrt(ms + RMS_EPS) * gain * one_minus_init
        y_ref[pl.ds(pl.multiple_of(i * BLOCK, BLOCK), BLOCK), :] = y.astype(BF16)

    nbuf = len(s_refs)
    group = DIFF_BLOCKS_PER_STEP if nq % DIFF_BLOCKS_PER_STEP == 0 else nbuf

    def block_group(t, m):
        for k in range(group):
            i = group * t + k
            m_next = scores(jnp.minimum(i + 1, nq - 1), s_refs[(k + 1) % nbuf])
            finish(i, s_refs[k % nbuf], m)
            m = m_next
        return m

    lax.fori_loop(0, nq // group, block_group, scores(0, s_refs[0]))


def _diff_attn(qd, kd, vd, lam_vec, prm, batch, seq):
    nq = seq // BLOCK
    tspec = pl.BlockSpec((1, nq, LANES, BLOCK), lambda b, h: (b, 0, h, 0))
    kspec = pl.BlockSpec((seq, LANES), lambda b, h: (b, h))
    small = lambda shape: pl.BlockSpec(shape, lambda b, h: (0, 0))
    return pl.pallas_call(
        functools.partial(_diff_attn_kernel, seq=seq),
        grid=(batch, DIFF_HEADS),
        in_specs=[tspec, kspec, tspec, small((4, HEAD_DIM)), small((8, LANES))],
        out_specs=pl.BlockSpec((seq, LANES), lambda b, h: (b, h)),
        out_shape=jax.ShapeDtypeStruct((batch * seq, DV), BF16),
        scratch_shapes=[pltpu.VMEM((seq, 2 * BLOCK), F32)] * DIFF_SCORE_BUFFERS,
        compiler_params=pltpu.CompilerParams(
            dimension_semantics=("parallel", "parallel"), vmem_limit_bytes=VMEM_LIMIT),
        name="diff_attn",
    )(qd, kd, vd, lam_vec, prm)


SWA_BLOCKS_PER_STEP = 8


def _swa_kernel(q_ref, k_ref, v_ref, sink_ref, y_ref, *, seq):
    nq = seq // BLOCK
    sel0 = _head_select_mask()
    r = lax.broadcasted_iota(jnp.int32, (BLOCK, BLOCK), 0)
    c = lax.broadcasted_iota(jnp.int32, (BLOCK, BLOCK), 1)
    prev_band = jnp.where(c <= r, 0.0, NEG).astype(F32)
    next_band = jnp.where(r <= c, 0.0, NEG).astype(F32)

    rows = lambda blk: pl.ds(pl.multiple_of(blk * BLOCK, BLOCK), BLOCK)
    unroll = SWA_BLOCKS_PER_STEP if nq % SWA_BLOCKS_PER_STEP == 0 else 1

    def load_block(n):
        n_prev = jnp.maximum(n - 1, 0)
        n_next = jnp.minimum(n + 1, nq - 1)
        edge_prev = jnp.where(n >= 1, 0.0, NEG).astype(F32)
        edge_next = jnp.where(n <= nq - 2, 0.0, NEG).astype(F32)
        bias_prev = jnp.concatenate([prev_band + edge_prev] * 2, axis=1)
        bias_next = jnp.concatenate([next_band + edge_next] * 2, axis=1)
        bias = (bias_prev, bias_next)
        kb = jnp.concatenate([k_ref[rows(n_prev), :], k_ref[rows(n), :], k_ref[rows(n_next), :]],
                             axis=0)
        vb = jnp.concatenate([v_ref[0, n_prev], v_ref[0, n], v_ref[0, n_next]], axis=1)
        qms = [_pair_queries(q_ref[0, n, g * LANES:(g + 1) * LANES, :], sel0)
               for g in range(SWA_GROUP)]
        return kb, vb, bias, qms

    def step(t, carry):
        blocks = [unroll * t + u for u in range(unroll)]
        loaded = [load_block(n) for n in blocks]
        def masked_scores(kb, qm, bias):
            s = _dot(kb, qm)
            return jnp.concatenate([s[:BLOCK] + bias[0], s[BLOCK:2 * BLOCK],
                                    s[2 * BLOCK:] + bias[1]], axis=0)

        scores = [[masked_scores(kb, qm, bias) for qm in qms] for kb, _, bias, qms in loaded]
        for n, (_, vb, _, _), block_scores in zip(blocks, loaded, scores):
            outs = []
            for g, s in enumerate(block_scores):
                sink = sink_ref[:, g * 2 * BLOCK:(g + 1) * 2 * BLOCK]
                m = jnp.maximum(jnp.max(s, axis=0, keepdims=True), sink)
                e = jnp.exp2(s - m)
                den = jnp.sum(e, axis=0, keepdims=True) + jnp.exp2(sink - m)
                on = _dot(vb, e.astype(BF16)) * (1.0 / den)
                yt = jnp.concatenate([on[:HEAD_DIM, :BLOCK], on[HEAD_DIM:, BLOCK:]], axis=0)
                outs.append(yt.T.astype(BF16))
            y_ref[rows(n), :] = jnp.concatenate(outs, axis=1)
        return carry

    lax.fori_loop(0, nq // unroll, step, 0)


def _swa_attn(qs, ks, vs, sink_vec, batch, seq):
    nq = seq // BLOCK
    return pl.pallas_call(
        functools.partial(_swa_kernel, seq=seq),
        grid=(batch,),
        in_specs=[pl.BlockSpec((1, nq, SQ, BLOCK), lambda b: (b, 0, 0, 0)),
                  pl.BlockSpec((seq, SK), lambda b: (b, 0)),
                  pl.BlockSpec((1, nq, SV, BLOCK), lambda b: (b, 0, 0, 0)),
                  pl.BlockSpec((1, SWA_Q_HEADS * BLOCK), lambda b: (0, 0))],
        out_specs=pl.BlockSpec((seq, SQ), lambda b: (b, 0)),
        out_shape=jax.ShapeDtypeStruct((batch * seq, SQ), BF16),
        compiler_params=pltpu.CompilerParams(
            dimension_semantics=("parallel",), vmem_limit_bytes=VMEM_LIMIT),
        name="swa_attn",
    )(qs, ks, vs, sink_vec)


def _token_tile(n, want):
    return want if n % want == 0 else BLOCK


def kernel(x, positions, w_in, w_out, diff_lambda, diff_subln_g, swa_sink,
           ffn1_w_in, ffn1_w_out, ffn2_w_in, ffn2_w_out, ln_g, ln_b):
    batch, seq, _ = x.shape
    n = batch * seq
    tm = _token_tile(seq, 1024)
    tm_ffn = _token_tile(n, FFN_TOKEN_TILE)
    h = x.reshape(n, D_MODEL)
    cos, sin = _rope_tables(positions)

    w_in_p = _permute_in_proj(w_in).astype(BF16)
    w_out_p = _permute_out_proj(w_out).astype(BF16)
    ffn1_in, ffn1_out = ffn1_w_in.astype(BF16), ffn1_w_out.astype(BF16)
    ffn2_in, ffn2_out = ffn2_w_in.astype(BF16), ffn2_w_out.astype(BF16)
    sink = swa_sink.reshape(DEPTH, SWA_KV_HEADS, SWA_GROUP).transpose(0, 2, 1) * LOG2E
    sink_vec = jnp.repeat(sink.reshape(DEPTH, 1, SWA_Q_HEADS), BLOCK, axis=2)

    for l in range(DEPTH):
        lambda_init = 0.8 - 0.6 * math.exp(-0.3 * l)
        g = lambda i: ln_g[l, i].reshape(1, D_MODEL)
        b = lambda i: ln_b[l, i].reshape(1, D_MODEL)
        prm = jnp.zeros((8, LANES), F32)
        prm = prm.at[0].set(diff_subln_g[l]).at[1].set(1.0 - lambda_init).at[2].set(lambda_init)

        h = _ffn_ln(h, ffn1_in, ffn1_out, g(0), b(0), l, tm_ffn)
        qd, kd, vd, qs, ks, vs = _in_proj(h, w_in_p, cos, sin, l, batch, seq, tm)
        yd = _diff_attn(qd, kd, vd, diff_lambda[l], prm, batch, seq)
        ys = _swa_attn(qs, ks, vs, sink_vec[l], batch, seq)
        h = _mix_ffn_ln(h, yd, ys, w_out_p, g(1), b(1), ffn2_in, ffn2_out, g(2), b(2), l, tm_ffn)
    return h.reshape(batch, seq, D_MODEL)
```

```python
import functools
import math

import jax
import jax.numpy as jnp
from jax import lax
from jax.experimental import pallas as pl
from jax.experimental.pallas import tpu as pltpu

D_MODEL = 1024
DEPTH = 4
HEAD_DIM = 64
HALF = HEAD_DIM // 2
DIFF_HEADS = 4
SWA_Q_HEADS = 8
SWA_KV_HEADS = 2
SWA_GROUP = SWA_Q_HEADS // SWA_KV_HEADS
BLOCK = 128
D_FF = 2816
ROPE_THETA = 10000.0
ALPHA = (2.0 * DEPTH) ** 0.25
LN_EPS = 1e-5
RMS_EPS = 1e-5
LOG2E = 1.4426950408889634
Q_SCALE = (HEAD_DIM ** -0.5) * LOG2E
NEG = -1e30

LANES = 128
SUBLANES = 8
DQ = DK = DV = 512
SQ = 512
SK = SV = 128
IN_COLS = DQ + DK + DV + SQ + SK + SV
ROPE_COLS = DQ + DK + SQ + SK

VMEM_LIMIT = 56 * 1024 * 1024

BF16 = jnp.bfloat16
F32 = jnp.float32


def _permute_in_proj(w):
    lead = w.shape[:-1]
    dq, dk, dv = w[..., :DQ], w[..., DQ:DQ + DK], w[..., DQ + DK:DQ + DK + DV]
    sq0 = DQ + DK + DV
    sq, sk, sv = w[..., sq0:sq0 + SQ], w[..., sq0 + SQ:sq0 + SQ + SK], w[..., sq0 + SQ + SK:]
    pair = lambda t: t.reshape(lead + (DIFF_HEADS, 2, 2, HALF)).swapaxes(-3, -2).reshape(lead + (DQ,))
    sq = jnp.moveaxis(sq.reshape(lead + (SWA_KV_HEADS, SWA_GROUP, 2, HALF)), -4, -2)
    sk = sk.reshape(lead + (SWA_KV_HEADS, 2, HALF)).swapaxes(-3, -2)
    return jnp.concatenate(
        [pair(dq), pair(dk), sq.reshape(lead + (SQ,)), sk.reshape(lead + (SK,)), dv, sv], axis=-1)


def _permute_out_proj(w):
    lead, tail = w.shape[:-2], w.shape[-1:]
    swa = w[..., DV:, :].reshape(lead + (SWA_KV_HEADS, SWA_GROUP, HEAD_DIM) + tail)
    swa = swa.swapaxes(-4, -3).reshape(lead + (SQ,) + tail)
    return jnp.concatenate([w[..., :DV, :], swa], axis=-2)


def _layer_norm(z, g, b):
    mu = jnp.mean(z, axis=-1, keepdims=True)
    zc = z - mu
    var = jnp.mean(zc * zc, axis=-1, keepdims=True)
    return zc * lax.rsqrt(var + LN_EPS) * g + b


def _dot(a, b):
    return jnp.dot(a, b, preferred_element_type=F32)


ROPE_PACK = LANES // HALF
ROPE_TILE = 2048


def _rope_table_kernel(pos_ref, inv_ref, sign_ref, cos_ref, sin_ref):
    rows = pos_ref.shape[0]
    ang = pos_ref[...] * inv_ref[...]
    cos = jnp.cos(ang)
    sin = jnp.sin(ang)
    first_group = lax.broadcasted_iota(jnp.int32, (rows, LANES), 1) < HALF

    def spread(t, j):
        x = t if j == 0 else pltpu.roll(t, LANES - HALF * j, 1)
        x = jnp.where(first_group, x, 0.0)
        x = x + pltpu.roll(x, HALF, 1)
        return x + pltpu.roll(x, 2 * HALF, 1)

    for j in range(ROPE_PACK):
        cos_ref[j * rows:(j + 1) * rows, :] = spread(cos, j)
        sin_ref[j * rows:(j + 1) * rows, :] = spread(sin, j) * sign_ref[...]


def _rope_tables(positions):
    n = positions.size
    tile = min(n, ROPE_TILE)
    rows = tile // ROPE_PACK
    inv = jnp.power(ROPE_THETA, -jnp.arange(0, HEAD_DIM, 2, dtype=F32) / HEAD_DIM)
    inv128 = jnp.tile(inv, ROPE_PACK).reshape(1, LANES)
    sign = jnp.where(jnp.arange(LANES) < LANES // 2, -1.0, 1.0).astype(F32).reshape(1, LANES)
    pos = positions.reshape(n // tile, ROPE_PACK, rows).astype(F32).transpose(0, 2, 1)
    pos = jnp.repeat(pos, HALF, axis=2).reshape(n // ROPE_PACK, LANES)
    packed = pl.BlockSpec((rows, LANES), lambda i: (i, 0))
    row = pl.BlockSpec((tile, LANES), lambda i: (i, 0))
    vec = pl.BlockSpec((1, LANES), lambda i: (0, 0))
    return pl.pallas_call(
        _rope_table_kernel,
        grid=(n // tile,),
        in_specs=[packed, vec, vec],
        out_specs=[row, row],
        out_shape=[jax.ShapeDtypeStruct((n, LANES), F32)] * 2,
        name="rope_tables",
    )(pos, inv128, sign)


FFN_CHUNK = 256


FFN_TOKEN_TILE = 1024
FFN_OUT_SPLITS = 4


def _ffn_residual_ln(x, wi_ref, wo_ref, g_ref, b_ref, o_ref, act_ref):
    xb = x.astype(BF16)
    for c in range(D_FF // FFN_CHUNK):
        lo = c * FFN_CHUNK
        gate = _dot(xb, wi_ref[:, lo:lo + FFN_CHUNK])
        up = _dot(xb, wi_ref[:, D_FF + lo:D_FF + lo + FFN_CHUNK])
        act = gate * (1.0 / (1.0 + jnp.exp(-gate))) * up
        act_ref[:, lo:lo + FFN_CHUNK] = act.astype(BF16)
    rows = x.shape[0] // FFN_OUT_SPLITS
    for r in range(FFN_OUT_SPLITS):
        sl = slice(r * rows, (r + 1) * rows)
        y = _dot(act_ref[sl, :], wo_ref[...])
        o_ref[sl, :] = _layer_norm(ALPHA * x[sl, :] + 0.5 * y, g_ref[...], b_ref[...])


def _ffn_kernel(x_ref, wi_ref, wo_ref, g_ref, b_ref, o_ref, act_ref):
    _ffn_residual_ln(x_ref[...], wi_ref, wo_ref, g_ref, b_ref, o_ref, act_ref)


def _const_spec(shape):
    return pl.BlockSpec(shape, lambda i: (0, 0), pipeline_mode=pl.Buffered(1))


def _layer_spec(shape, layer):
    return pl.BlockSpec((None,) + shape, lambda i: (layer, 0, 0), pipeline_mode=pl.Buffered(1))


def _ffn_ln(x, w_in, w_out, g, b, layer, tm):
    n = x.shape[0]
    row = pl.BlockSpec((tm, D_MODEL), lambda i: (i, 0))
    return pl.pallas_call(
        _ffn_kernel,
        grid=(n // tm,),
        in_specs=[row, _layer_spec((D_MODEL, 2 * D_FF), layer), _layer_spec((D_FF, D_MODEL), layer),
                  _const_spec((1, D_MODEL)), _const_spec((1, D_MODEL))],
        out_specs=row,
        out_shape=jax.ShapeDtypeStruct((n, D_MODEL), F32),
        scratch_shapes=[pltpu.VMEM((tm, D_FF), BF16)],
        compiler_params=pltpu.CompilerParams(
            dimension_semantics=("parallel",), vmem_limit_bytes=VMEM_LIMIT),
        name="ffn_ln",
    )(x, w_in, w_out, g, b)


def _mix_ffn_kernel(x_ref, yd_ref, ys_ref, wm_ref, gm_ref, bm_ref,
                    wi_ref, wo_ref, g_ref, b_ref, o_ref, act_ref):
    mix = _dot(yd_ref[...], wm_ref[:DV, :]) + _dot(ys_ref[...], wm_ref[DV:, :])
    x = _layer_norm(ALPHA * x_ref[...] + mix, gm_ref[...], bm_ref[...])
    _ffn_residual_ln(x, wi_ref, wo_ref, g_ref, b_ref, o_ref, act_ref)


def _mix_ffn_ln(x, yd, ys, w_mix, g_mix, b_mix, w_in, w_out, g, b, layer, tm):
    n = x.shape[0]
    row = lambda width: pl.BlockSpec((tm, width), lambda i: (i, 0))
    vec = _const_spec((1, D_MODEL))
    return pl.pallas_call(
        _mix_ffn_kernel,
        grid=(n // tm,),
        in_specs=[row(D_MODEL), row(DV), row(SQ), _layer_spec((DV + SQ, D_MODEL), layer), vec, vec,
                  _layer_spec((D_MODEL, 2 * D_FF), layer), _layer_spec((D_FF, D_MODEL), layer),
                  vec, vec],
        out_specs=row(D_MODEL),
        out_shape=jax.ShapeDtypeStruct((n, D_MODEL), F32),
        scratch_shapes=[pltpu.VMEM((tm, D_FF), BF16)],
        compiler_params=pltpu.CompilerParams(
            dimension_semantics=("parallel",), vmem_limit_bytes=VMEM_LIMIT),
        name="mix_ffn_ln",
    )(x, yd, ys, w_mix, g_mix, b_mix, w_in, w_out, g, b)


def _proj_kernel(x_ref, w_ref, cos_ref, sin_ref,
                 qd_ref, kd_ref, vd_ref, qs_ref, ks_ref, vs_ref, *, tm):
    xb = x_ref[...].astype(BF16)
    cos = cos_ref[...]
    sin = sin_ref[...]

    def rope(t, scale):
        r = t * cos + pltpu.roll(t, LANES // 2, 1) * sin
        return r * scale if scale != 1.0 else r

    def put_transposed(ref, col, t):
        for blk in range(tm // BLOCK):
            ref[0, blk, col:col + LANES, :] = t[blk * BLOCK:(blk + 1) * BLOCK, :].T.astype(BF16)

    def put_rows(ref, col, t):
        ref[:, col:col + LANES] = t.astype(BF16)

    plan = []
    for j in range(DQ // LANES):
        plan.append((j * LANES, qd_ref, j * LANES, put_transposed, Q_SCALE))
    for j in range(DK // LANES):
        plan.append((DQ + j * LANES, kd_ref, j * LANES, put_rows, 1.0))
    for j in range(SQ // LANES):
        plan.append((DQ + DK + j * LANES, qs_ref, j * LANES, put_transposed, Q_SCALE))
    plan.append((DQ + DK + SQ, ks_ref, 0, put_rows, 1.0))
    for j in range(DV // LANES):
        plan.append((ROPE_COLS + j * LANES, vd_ref, j * LANES, put_transposed, None))
    plan.append((ROPE_COLS + DV, vs_ref, 0, put_transposed, None))

    for first in range(0, len(plan), 2):
        lo = plan[first][0]
        t2 = _dot(xb, w_ref[:, lo:lo + 2 * LANES])
        for half, (_, ref, col, put, scale) in enumerate(plan[first:first + 2]):
            t = t2[:, half * LANES:(half + 1) * LANES]
            put(ref, col, t if scale is None else rope(t, scale))


def _in_proj(x, w, cos, sin, layer, batch, seq, tm):
    n = x.shape[0]
    nq = seq // BLOCK
    tpb = seq // tm
    sub = tm // BLOCK
    row = lambda width: pl.BlockSpec((tm, width), lambda i: (i, 0))
    tblk = lambda rows: pl.BlockSpec((1, sub, rows, BLOCK), lambda i: (i // tpb, i % tpb, 0, 0))
    tshape = lambda rows: jax.ShapeDtypeStruct((batch, nq, rows, BLOCK), BF16)
    return pl.pallas_call(
        functools.partial(_proj_kernel, tm=tm),
        grid=(n // tm,),
        in_specs=[row(D_MODEL), _layer_spec((D_MODEL, IN_COLS), layer), row(LANES), row(LANES)],
        out_specs=[tblk(DQ), row(DK), tblk(DV), tblk(SQ), row(SK), tblk(SV)],
        out_shape=[tshape(DQ), jax.ShapeDtypeStruct((n, DK), BF16), tshape(DV),
                   tshape(SQ), jax.ShapeDtypeStruct((n, SK), BF16), tshape(SV)],
        compiler_params=pltpu.CompilerParams(
            dimension_semantics=("parallel",), vmem_limit_bytes=VMEM_LIMIT),
        name="in_proj_rope",
    )(x, w, cos, sin)


def _head_select_mask():
    r = lax.broadcasted_iota(jnp.int32, (LANES, 1), 0)
    return (r % HEAD_DIM) < HALF


def _pair_queries(qt, sel0):
    zero = jnp.zeros_like(qt)
    return jnp.concatenate([jnp.where(sel0, qt, zero), jnp.where(sel0, zero, qt)], axis=1)


DIFF_KV_TILE = 256
DIFF_SCORE_BUFFERS = 4
DIFF_BLOCKS_PER_STEP = 8


def _diff_attn_kernel(q_ref, k_ref, v_ref, lam_ref, prm_ref, y_ref, *s_refs, seq):
    lv = lam_ref[...]
    a1 = jnp.sum(lv[0:1, :] * lv[1:2, :], axis=-1, keepdims=True)
    a2 = jnp.sum(lv[2:3, :] * lv[3:4, :], axis=-1, keepdims=True)
    lam = jnp.exp(a1) - jnp.exp(a2) + prm_ref[2:3, 0:1]
    gain = prm_ref[0:1, :]
    one_minus_init = prm_ref[1:2, :]
    sel0 = _head_select_mask()
    tk = min(DIFF_KV_TILE, seq)
    nq = seq // BLOCK
    tiles = [(j * tk, (j + 1) * tk) for j in range(seq // tk)]

    def scores(i, s_ref):
        qm = _pair_queries(q_ref[0, i], sel0)
        m8 = None
        for lo, hi in tiles:
            s = _dot(k_ref[lo:hi, :], qm)
            s_ref[lo:hi, :] = s
            tile_max = jnp.max(s.reshape(tk // SUBLANES, SUBLANES, 2 * BLOCK), axis=0)
            m8 = tile_max if m8 is None else jnp.maximum(m8, tile_max)
        return jnp.max(m8, axis=0, keepdims=True)

    def finish(i, s_ref, m):
        l8 = jnp.zeros((SUBLANES, 2 * BLOCK), F32)
        o = jnp.zeros((LANES, 2 * BLOCK), F32)
        for lo, hi in tiles:
            p = jnp.exp2(s_ref[lo:hi, :] - m)
            l8 = l8 + jnp.sum(p.reshape(tk // SUBLANES, SUBLANES, 2 * BLOCK), axis=0)
            vt = jnp.concatenate([v_ref[0, t] for t in range(lo // BLOCK, hi // BLOCK)], axis=1)
            o = o + _dot(vt, p.astype(BF16))
        on = o * (1.0 / jnp.sum(l8, axis=0, keepdims=True))
        d = (on[:, :BLOCK] - lam * on[:, BLOCK:]).T
        ms = jnp.mean(d * d, axis=-1, keepdims=True)
        y = d * lax.rsqrt(ms + RMS_EPS) * gain * one_minus_init
        start = i * BLOCK if isinstance(i, int) else pl.multiple_of(i * BLOCK, BLOCK)
        y_ref[pl.ds(start, BLOCK), :] = y.astype(BF16)

    nbuf = len(s_refs)
    group = DIFF_BLOCKS_PER_STEP if nq % DIFF_BLOCKS_PER_STEP == 0 else nbuf

    def block_group(t, m, last=False):
        for k in range(group):
            i = group * t + k
            if not (last and k == group - 1):
                m_next = scores(i + 1, s_refs[(k + 1) % nbuf])
            finish(i, s_refs[k % nbuf], m)
            m = m_next
        return m

    steps = nq // group
    m = lax.fori_loop(0, steps - 1, block_group, scores(0, s_refs[0]))
    block_group(steps - 1, m, last=True)


def _diff_attn(qd, kd, vd, lam_vec, prm, batch, seq):
    nq = seq // BLOCK
    tspec = pl.BlockSpec((1, nq, LANES, BLOCK), lambda b, h: (b, 0, h, 0))
    kspec = pl.BlockSpec((seq, LANES), lambda b, h: (b, h))
    small = lambda shape: pl.BlockSpec(shape, lambda b, h: (0, 0))
    return pl.pallas_call(
        functools.partial(_diff_attn_kernel, seq=seq),
        grid=(batch, DIFF_HEADS),
        in_specs=[tspec, kspec, tspec, small((4, HEAD_DIM)), small((8, LANES))],
        out_specs=pl.BlockSpec((seq, LANES), lambda b, h: (b, h)),
        out_shape=jax.ShapeDtypeStruct((batch * seq, DV), BF16),
        scratch_shapes=[pltpu.VMEM((seq, 2 * BLOCK), F32)] * DIFF_SCORE_BUFFERS,
        compiler_params=pltpu.CompilerParams(
            dimension_semantics=("parallel", "parallel"), vmem_limit_bytes=VMEM_LIMIT),
        name="diff_attn",
    )(qd, kd, vd, lam_vec, prm)


SWA_BLOCKS_PER_STEP = 8


def _swa_kernel(q_ref, k_ref, v_ref, sink_ref, y_ref, *, seq):
    nq = seq // BLOCK
    sel0 = _head_select_mask()
    r = lax.broadcasted_iota(jnp.int32, (BLOCK, BLOCK), 0)
    c = lax.broadcasted_iota(jnp.int32, (BLOCK, BLOCK), 1)
    prev_band = jnp.where(c <= r, 0.0, NEG).astype(F32)
    next_band = jnp.where(r <= c, 0.0, NEG).astype(F32)

    rows = lambda blk: pl.ds(pl.multiple_of(blk * BLOCK, BLOCK), BLOCK)
    unroll = SWA_BLOCKS_PER_STEP if nq % SWA_BLOCKS_PER_STEP == 0 else 1

    def load_block(n):
        n_prev = jnp.maximum(n - 1, 0)
        n_next = jnp.minimum(n + 1, nq - 1)
        edge_prev = jnp.where(n >= 1, 0.0, NEG).astype(F32)
        edge_next = jnp.where(n <= nq - 2, 0.0, NEG).astype(F32)
        bias_prev = jnp.concatenate([prev_band + edge_prev] * 2, axis=1)
        bias_next = jnp.concatenate([next_band + edge_next] * 2, axis=1)
        bias = (bias_prev, bias_next)
        kb = jnp.concatenate([k_ref[rows(n_prev), :], k_ref[rows(n), :], k_ref[rows(n_next), :]],
                             axis=0)
        vb = jnp.concatenate([v_ref[0, n_prev], v_ref[0, n], v_ref[0, n_next]], axis=1)
        qms = [_pair_queries(q_ref[0, n, g * LANES:(g + 1) * LANES, :], sel0)
               for g in range(SWA_GROUP)]
        return kb, vb, bias, qms

    def step(t, carry):
        blocks = [unroll * t + u for u in range(unroll)]
        loaded = [load_block(n) for n in blocks]
        def masked_scores(kb, qm, bias):
            s = _dot(kb, qm)
            return jnp.concatenate([s[:BLOCK] + bias[0], s[BLOCK:2 * BLOCK],
                                    s[2 * BLOCK:] + bias[1]], axis=0)

        scores = [[masked_scores(kb, qm, bias) for qm in qms] for kb, _, bias, qms in loaded]
        for n, (_, vb, _, _), block_scores in zip(blocks, loaded, scores):
            outs = []
            for g, s in enumerate(block_scores):
                sink = sink_ref[:, g * 2 * BLOCK:(g + 1) * 2 * BLOCK]
                m = jnp.maximum(jnp.max(s, axis=0, keepdims=True), sink)
                e = jnp.exp2(s - m)
                den = jnp.sum(e, axis=0, keepdims=True) + jnp.exp2(sink - m)
                on = _dot(vb, e.astype(BF16)) * (1.0 / den)
                yt = jnp.concatenate([on[:HEAD_DIM, :BLOCK], on[HEAD_DIM:, BLOCK:]], axis=0)
                outs.append(yt.T.astype(BF16))
            y_ref[rows(n), :] = jnp.concatenate(outs, axis=1)
        return carry

    lax.fori_loop(0, nq // unroll, step, 0)


def _swa_attn(qs, ks, vs, sink_vec, batch, seq):
    nq = seq // BLOCK
    return pl.pallas_call(
        functools.partial(_swa_kernel, seq=seq),
        grid=(batch,),
        in_specs=[pl.BlockSpec((1, nq, SQ, BLOCK), lambda b: (b, 0, 0, 0)),
                  pl.BlockSpec((seq, SK), lambda b: (b, 0)),
                  pl.BlockSpec((1, nq, SV, BLOCK), lambda b: (b, 0, 0, 0)),
                  pl.BlockSpec((1, SWA_Q_HEADS * BLOCK), lambda b: (0, 0))],
        out_specs=pl.BlockSpec((seq, SQ), lambda b: (b, 0)),
        out_shape=jax.ShapeDtypeStruct((batch * seq, SQ), BF16),
        compiler_params=pltpu.CompilerParams(
            dimension_semantics=("parallel",), vmem_limit_bytes=VMEM_LIMIT),
        name="swa_attn",
    )(qs, ks, vs, sink_vec)


def _token_tile(n, want):
    return want if n % want == 0 else BLOCK


def kernel(x, positions, w_in, w_out, diff_lambda, diff_subln_g, swa_sink,
           ffn1_w_in, ffn1_w_out, ffn2_w_in, ffn2_w_out, ln_g, ln_b):
    batch, seq, _ = x.shape
    n = batch * seq
    tm = _token_tile(seq, 1024)
    tm_ffn = _token_tile(n, FFN_TOKEN_TILE)
    h = x.reshape(n, D_MODEL)
    cos, sin = _rope_tables(positions)

    w_in_p = _permute_in_proj(w_in).astype(BF16)
    w_out_p = _permute_out_proj(w_out).astype(BF16)
    ffn1_in, ffn1_out = ffn1_w_in.astype(BF16), ffn1_w_out.astype(BF16)
    ffn2_in, ffn2_out = ffn2_w_in.astype(BF16), ffn2_w_out.astype(BF16)
    sink = swa_sink.reshape(DEPTH, SWA_KV_HEADS, SWA_GROUP).transpose(0, 2, 1) * LOG2E
    sink_vec = jnp.repeat(sink.reshape(DEPTH, 1, SWA_Q_HEADS), BLOCK, axis=2)

    for l in range(DEPTH):
        lambda_init = 0.8 - 0.6 * math.exp(-0.3 * l)
        g = lambda i: ln_g[l, i].reshape(1, D_MODEL)
        b = lambda i: ln_b[l, i].reshape(1, D_MODEL)
        prm = jnp.zeros((8, LANES), F32)
        prm = prm.at[0].set(diff_subln_g[l]).at[1].set(1.0 - lambda_init).at[2].set(lambda_init)

        h = _ffn_ln(h, ffn1_in, ffn1_out, g(0), b(0), l, tm_ffn)
        qd, kd, vd, qs, ks, vs = _in_proj(h, w_in_p, cos, sin, l, batch, seq, tm)
        yd = _diff_attn(qd, kd, vd, diff_lambda[l], prm, batch, seq)
        ys = _swa_attn(qs, ks, vs, sink_vec[l], batch, seq)
        h = _mix_ffn_ln(h, yd, ys, w_out_p, g(1), b(1), ffn2_in, ffn2_out, g(2), b(2), l, tm_ffn)
    return h.reshape(batch, seq, D_MODEL)
```

```python
import functools
import math

import jax
import jax.numpy as jnp
from jax import lax
from jax.experimental import pallas as pl
from jax.experimental.pallas import tpu as pltpu

D_MODEL = 1024
DEPTH = 4
HEAD_DIM = 64
HALF = HEAD_DIM // 2
DIFF_HEADS = 4
SWA_Q_HEADS = 8
SWA_KV_HEADS = 2
SWA_GROUP = SWA_Q_HEADS // SWA_KV_HEADS
BLOCK = 128
D_FF = 2816
ROPE_THETA = 10000.0
ALPHA = (2.0 * DEPTH) ** 0.25
LN_EPS = 1e-5
RMS_EPS = 1e-5
LOG2E = 1.4426950408889634
Q_SCALE = (HEAD_DIM ** -0.5) * LOG2E
NEG = -1e30

LANES = 128
SUBLANES = 8
DQ = DK = DV = 512
SQ = 512
SK = SV = 128
IN_COLS = DQ + DK + DV + SQ + SK + SV
ROPE_COLS = DQ + DK + SQ + SK

VMEM_LIMIT = 56 * 1024 * 1024

BF16 = jnp.bfloat16
F32 = jnp.float32


def _permute_in_proj(w):
    lead = w.shape[:-1]
    dq, dk, dv = w[..., :DQ], w[..., DQ:DQ + DK], w[..., DQ + DK:DQ + DK + DV]
    sq0 = DQ + DK + DV
    sq, sk, sv = w[..., sq0:sq0 + SQ], w[..., sq0 + SQ:sq0 + SQ + SK], w[..., sq0 + SQ + SK:]
    pair = lambda t: t.reshape(lead + (DIFF_HEADS, 2, 2, HALF)).swapaxes(-3, -2).reshape(lead + (DQ,))
    sq = jnp.moveaxis(sq.reshape(lead + (SWA_KV_HEADS, SWA_GROUP, 2, HALF)), -4, -2)
    sk = sk.reshape(lead + (SWA_KV_HEADS, 2, HALF)).swapaxes(-3, -2)
    return jnp.concatenate(
        [pair(dq), pair(dk), sq.reshape(lead + (SQ,)), sk.reshape(lead + (SK,)), dv, sv], axis=-1)


def _permute_out_proj(w):
    lead, tail = w.shape[:-2], w.shape[-1:]
    swa = w[..., DV:, :].reshape(lead + (SWA_KV_HEADS, SWA_GROUP, HEAD_DIM) + tail)
    swa = swa.swapaxes(-4, -3).reshape(lead + (SQ,) + tail)
    return jnp.concatenate([w[..., :DV, :], swa], axis=-2)


def _layer_norm(z, g, b):
    mu = jnp.mean(z, axis=-1, keepdims=True)
    zc = z - mu
    var = jnp.mean(zc * zc, axis=-1, keepdims=True)
    return zc * lax.rsqrt(var + LN_EPS) * g + b


def _dot(a, b):
    return jnp.dot(a, b, preferred_element_type=F32)


ROPE_PACK = LANES // HALF
ROPE_TILE = 2048


def _rope_table_kernel(pos_ref, inv_ref, sign_ref, cos_ref, sin_ref):
    rows = pos_ref.shape[0]
    ang = pos_ref[...] * inv_ref[...]
    cos = jnp.cos(ang)
    sin = jnp.sin(ang)
    first_group = lax.broadcasted_iota(jnp.int32, (rows, LANES), 1) < HALF

    def spread(t, j):
        x = t if j == 0 else pltpu.roll(t, LANES - HALF * j, 1)
        x = jnp.where(first_group, x, 0.0)
        x = x + pltpu.roll(x, HALF, 1)
        return x + pltpu.roll(x, 2 * HALF, 1)

    for j in range(ROPE_PACK):
        cos_ref[j * rows:(j + 1) * rows, :] = spread(cos, j)
        sin_ref[j * rows:(j + 1) * rows, :] = spread(sin, j) * sign_ref[...]


def _rope_tables(positions):
    n = positions.size
    tile = min(n, ROPE_TILE)
    rows = tile // ROPE_PACK
    inv = jnp.power(ROPE_THETA, -jnp.arange(0, HEAD_DIM, 2, dtype=F32) / HEAD_DIM)
    inv128 = jnp.tile(inv, ROPE_PACK).reshape(1, LANES)
    sign = jnp.where(jnp.arange(LANES) < LANES // 2, -1.0, 1.0).astype(F32).reshape(1, LANES)
    pos = positions.reshape(n // tile, ROPE_PACK, rows).astype(F32).transpose(0, 2, 1)
    pos = jnp.repeat(pos, HALF, axis=2).reshape(n // ROPE_PACK, LANES)
    packed = pl.BlockSpec((rows, LANES), lambda i: (i, 0))
    row = pl.BlockSpec((tile, LANES), lambda i: (i, 0))
    vec = pl.BlockSpec((1, LANES), lambda i: (0, 0))
    return pl.pallas_call(
        _rope_table_kernel,
        grid=(n // tile,),
        in_specs=[packed, vec, vec],
        out_specs=[row, row],
        out_shape=[jax.ShapeDtypeStruct((n, LANES), F32)] * 2,
        name="rope_tables",
    )(pos, inv128, sign)


FFN_CHUNK = 256


FFN_TOKEN_TILE = 1024
FFN_OUT_SPLITS = 4


def _ffn_residual_ln(x, wi_ref, wo_ref, g_ref, b_ref, o_ref, act_ref):
    xb = x.astype(BF16)
    for c in range(D_FF // FFN_CHUNK):
        lo = c * FFN_CHUNK
        gate = _dot(xb, wi_ref[:, lo:lo + FFN_CHUNK])
        up = _dot(xb, wi_ref[:, D_FF + lo:D_FF + lo + FFN_CHUNK])
        act = gate * (1.0 / (1.0 + jnp.exp(-gate))) * up
        act_ref[:, lo:lo + FFN_CHUNK] = act.astype(BF16)
    rows = x.shape[0] // FFN_OUT_SPLITS
    for r in range(FFN_OUT_SPLITS):
        sl = slice(r * rows, (r + 1) * rows)
        y = _dot(act_ref[sl, :], wo_ref[...])
        o_ref[sl, :] = _layer_norm(ALPHA * x[sl, :] + 0.5 * y, g_ref[...], b_ref[...])


def _ffn_kernel(x_ref, wi_ref, wo_ref, g_ref, b_ref, o_ref, act_ref):
    _ffn_residual_ln(x_ref[...], wi_ref, wo_ref, g_ref, b_ref, o_ref, act_ref)


def _const_spec(shape):
    return pl.BlockSpec(shape, lambda i: (0, 0), pipeline_mode=pl.Buffered(1))


def _layer_spec(shape, layer):
    return pl.BlockSpec((None,) + shape, lambda i: (layer, 0, 0), pipeline_mode=pl.Buffered(1))


def _ffn_ln(x, w_in, w_out, g, b, layer, tm):
    n = x.shape[0]
    row = pl.BlockSpec((tm, D_MODEL), lambda i: (i, 0))
    return pl.pallas_call(
        _ffn_kernel,
        grid=(n // tm,),
        in_specs=[row, _layer_spec((D_MODEL, 2 * D_FF), layer), _layer_spec((D_FF, D_MODEL), layer),
                  _const_spec((1, D_MODEL)), _const_spec((1, D_MODEL))],
        out_specs=row,
        out_shape=jax.ShapeDtypeStruct((n, D_MODEL), F32),
        scratch_shapes=[pltpu.VMEM((tm, D_FF), BF16)],
        compiler_params=pltpu.CompilerParams(
            dimension_semantics=("parallel",), vmem_limit_bytes=VMEM_LIMIT),
        name="ffn_ln",
    )(x, w_in, w_out, g, b)


def _mix_ffn_kernel(x_ref, yd_ref, ys_ref, wm_ref, gm_ref, bm_ref,
                    wi_ref, wo_ref, g_ref, b_ref, o_ref, act_ref):
    mix = _dot(jnp.concatenate([yd_ref[...], ys_ref[...]], axis=1), wm_ref[...])
    x = _layer_norm(ALPHA * x_ref[...] + mix, gm_ref[...], bm_ref[...])
    _ffn_residual_ln(x, wi_ref, wo_ref, g_ref, b_ref, o_ref, act_ref)


def _mix_ffn_ln(x, yd, ys, w_mix, g_mix, b_mix, w_in, w_out, g, b, layer, tm):
    n = x.shape[0]
    row = lambda width: pl.BlockSpec((tm, width), lambda i: (i, 0))
    vec = _const_spec((1, D_MODEL))
    return pl.pallas_call(
        _mix_ffn_kernel,
        grid=(n // tm,),
        in_specs=[row(D_MODEL), row(DV), row(SQ), _layer_spec((DV + SQ, D_MODEL), layer), vec, vec,
                  _layer_spec((D_MODEL, 2 * D_FF), layer), _layer_spec((D_FF, D_MODEL), layer),
                  vec, vec],
        out_specs=row(D_MODEL),
        out_shape=jax.ShapeDtypeStruct((n, D_MODEL), F32),
        scratch_shapes=[pltpu.VMEM((tm, D_FF), BF16)],
        compiler_params=pltpu.CompilerParams(
            dimension_semantics=("parallel",), vmem_limit_bytes=VMEM_LIMIT),
        name="mix_ffn_ln",
    )(x, yd, ys, w_mix, g_mix, b_mix, w_in, w_out, g, b)


def _proj_kernel(x_ref, w_ref, cos_ref, sin_ref, ones_ref,
                 qd_ref, kd_ref, vd_ref, qs_ref, ks_ref, vs_ref, norm_ref, *, tm):
    xb = x_ref[...].astype(BF16)
    cos = cos_ref[...]
    sin = sin_ref[...]

    def rope(t, scale):
        r = t * cos + pltpu.roll(t, LANES // 2, 1) * sin
        return r * scale if scale != 1.0 else r

    def put_transposed(ref, col, t):
        for blk in range(tm // BLOCK):
            ref[0, blk, col:col + LANES, :] = t[blk * BLOCK:(blk + 1) * BLOCK, :].T.astype(BF16)

    def put_rows(ref, col, t):
        ref[:, col:col + LANES] = t.astype(BF16)

    plan = []
    for j in range(DQ // LANES):
        plan.append((j * LANES, qd_ref, j * LANES, put_transposed, Q_SCALE))
    for j in range(DK // LANES):
        plan.append((DQ + j * LANES, kd_ref, j * LANES, put_rows, 1.0))
    for j in range(SQ // LANES):
        plan.append((DQ + DK + j * LANES, qs_ref, j * LANES, put_transposed, Q_SCALE))
    plan.append((DQ + DK + SQ, ks_ref, 0, put_rows, 1.0))
    for j in range(DV // LANES):
        plan.append((ROPE_COLS + j * LANES, vd_ref, j * LANES, put_transposed, None))
    plan.append((ROPE_COLS + DV, vs_ref, 0, put_transposed, None))

    squares = [None] * (2 * DIFF_HEADS)

    for first in range(0, len(plan), 2):
        lo = plan[first][0]
        t2 = _dot(xb, w_ref[:, lo:lo + 2 * LANES])
        for half, (_, ref, col, put, scale) in enumerate(plan[first:first + 2]):
            t = t2[:, half * LANES:(half + 1) * LANES]
            r = t if scale is None else rope(t, scale)
            put(ref, col, r)
            if ref is qd_ref or ref is kd_ref:
                slot = col // LANES + (0 if ref is qd_ref else DIFF_HEADS)
                squares[slot] = (r * r).astype(BF16)

    norms = _dot(jnp.concatenate(squares, axis=1), ones_ref[...])
    norm_ref[...] = jnp.broadcast_to(jnp.max(norms, axis=0, keepdims=True), (SUBLANES, LANES))[None]


def _in_proj(x, w, cos, sin, layer, batch, seq, tm):
    n = x.shape[0]
    nq = seq // BLOCK
    tpb = seq // tm
    sub = tm // BLOCK
    row = lambda width: pl.BlockSpec((tm, width), lambda i: (i, 0))
    tblk = lambda rows: pl.BlockSpec((1, sub, rows, BLOCK), lambda i: (i // tpb, i % tpb, 0, 0))
    tshape = lambda rows: jax.ShapeDtypeStruct((batch, nq, rows, BLOCK), BF16)
    group_ones = (jnp.arange(2 * DIFF_HEADS * LANES)[:, None] // LANES
                  == jnp.arange(LANES)[None, :]).astype(BF16)
    *proj, norms = pl.pallas_call(
        functools.partial(_proj_kernel, tm=tm),
        grid=(n // tm,),
        in_specs=[row(D_MODEL), _layer_spec((D_MODEL, IN_COLS), layer), row(LANES), row(LANES),
                  _const_spec((2 * DIFF_HEADS * LANES, LANES))],
        out_specs=[tblk(DQ), row(DK), tblk(DV), tblk(SQ), row(SK), tblk(SV),
                   pl.BlockSpec((1, SUBLANES, LANES), lambda i: (i, 0, 0))],
        out_shape=[tshape(DQ), jax.ShapeDtypeStruct((n, DK), BF16), tshape(DV),
                   tshape(SQ), jax.ShapeDtypeStruct((n, SK), BF16), tshape(SV),
                   jax.ShapeDtypeStruct((n // tm, SUBLANES, LANES), F32)],
        compiler_params=pltpu.CompilerParams(
            dimension_semantics=("parallel",), vmem_limit_bytes=VMEM_LIMIT),
        name="in_proj_rope",
    )(x, w, cos, sin, group_ones)
    biggest = norms[:, 0, :2 * DIFF_HEADS].reshape(batch, tpb, 2 * DIFF_HEADS).max(axis=1)
    bound = jnp.sqrt(biggest[:, :DIFF_HEADS] * biggest[:, DIFF_HEADS:]) * BF16_ROUNDING_MARGIN
    return proj, bound


def _head_select_mask():
    r = lax.broadcasted_iota(jnp.int32, (LANES, 1), 0)
    return (r % HEAD_DIM) < HALF


def _pair_queries(qt, sel0):
    zero = jnp.zeros_like(qt)
    return jnp.concatenate([jnp.where(sel0, qt, zero), jnp.where(sel0, zero, qt)], axis=1)


DIFF_KV_TILE = 256
DIFF_SCORE_BUFFERS = 4
DIRECT_LOOKAHEAD = 6
DIRECT_SCORE_LIMIT = 64.0
BF16_ROUNDING_MARGIN = 1.02
DIFF_BLOCKS_PER_STEP = 8


def _diff_attn_kernel(bound_ref, q_ref, k_ref, v_ref, lam_ref, prm_ref, y_ref, *s_refs, seq):
    lv = lam_ref[...]
    a1 = jnp.sum(lv[0:1, :] * lv[1:2, :], axis=-1, keepdims=True)
    a2 = jnp.sum(lv[2:3, :] * lv[3:4, :], axis=-1, keepdims=True)
    lam = jnp.exp(a1) - jnp.exp(a2) + prm_ref[2:3, 0:1]
    gain = prm_ref[0:1, :]
    one_minus_init = prm_ref[1:2, :]
    sel0 = _head_select_mask()
    tk = min(DIFF_KV_TILE, seq)
    nq = seq // BLOCK
    tiles = [(j * tk, (j + 1) * tk) for j in range(seq // tk)]

    def scores(i, s_ref):
        qm = _pair_queries(q_ref[0, i], sel0)
        m8 = None
        for lo, hi in tiles:
            s = _dot(k_ref[lo:hi, :], qm)
            s_ref[lo:hi, :] = s
            tile_max = jnp.max(s.reshape(tk // SUBLANES, SUBLANES, 2 * BLOCK), axis=0)
            m8 = tile_max if m8 is None else jnp.maximum(m8, tile_max)
        return jnp.max(m8, axis=0, keepdims=True)

    def weighted_values(p, lo, hi, o, l8):
        l8 = l8 + jnp.sum(p.reshape(tk // SUBLANES, SUBLANES, 2 * BLOCK), axis=0)
        vt = jnp.concatenate([v_ref[0, t] for t in range(lo // BLOCK, hi // BLOCK)], axis=1)
        return o + _dot(vt, p.astype(BF16)), l8

    def finish(i, s_ref, m):
        l8 = jnp.zeros((SUBLANES, 2 * BLOCK), F32)
        o = jnp.zeros((LANES, 2 * BLOCK), F32)
        for lo, hi in tiles:
            o, l8 = weighted_values(jnp.exp2(s_ref[lo:hi, :] - m), lo, hi, o, l8)
        write_block(i, o, l8)

    def write_block(i, o, l8):
        on = o * (1.0 / jnp.sum(l8, axis=0, keepdims=True))
        d = (on[:, :BLOCK] - lam * on[:, BLOCK:]).T
        ms = jnp.mean(d * d, axis=-1, keepdims=True)
        y = d * lax.rsqrt(ms + RMS_EPS) * gain * one_minus_init
        start = i * BLOCK if isinstance(i, int) else pl.multiple_of(i * BLOCK, BLOCK)
        y_ref[pl.ds(start, BLOCK), :] = y.astype(BF16)

    nbuf = len(s_refs)
    group = DIFF_BLOCKS_PER_STEP if nq % DIFF_BLOCKS_PER_STEP == 0 else nbuf

    def block_group(t, m, last=False):
        for k in range(group):
            i = group * t + k
            if not (last and k == group - 1):
                m_next = scores(i + 1, s_refs[(k + 1) % nbuf])
            finish(i, s_refs[k % nbuf], m)
            m = m_next
        return m

    steps = nq // group

    def shifted_path():
        m = lax.fori_loop(0, steps - 1, block_group, scores(0, s_refs[0]))
        block_group(steps - 1, m, last=True)

    def direct_path():
        def direct_group(t, carry):
            blocks = [group * t + k for k in range(group)]
            qms = [None] * group
            acc = [None] * group
            pending = []

            def consume():
                k, lo, hi, s = pending.pop(0)
                o, l8 = acc[k] if acc[k] is not None else (
                    jnp.zeros((LANES, 2 * BLOCK), F32), jnp.zeros((SUBLANES, 2 * BLOCK), F32))
                acc[k] = weighted_values(jnp.exp2(s), lo, hi, o, l8)
                if hi == seq:
                    write_block(blocks[k], *acc[k])

            for k in range(group):
                qms[k] = _pair_queries(q_ref[0, blocks[k]], sel0)
                for lo, hi in tiles:
                    pending.append((k, lo, hi, _dot(k_ref[lo:hi, :], qms[k])))
                    if len(pending) > DIRECT_LOOKAHEAD:
                        consume()
            while pending:
                consume()
            return carry
        lax.fori_loop(0, steps, direct_group, 0)

    bound = bound_ref[pl.program_id(0), pl.program_id(1)]
    lax.cond(bound < DIRECT_SCORE_LIMIT, direct_path, shifted_path)


def _diff_attn(bound, qd, kd, vd, lam_vec, prm, batch, seq):
    nq = seq // BLOCK
    tspec = pl.BlockSpec((1, nq, LANES, BLOCK), lambda b, h: (b, 0, h, 0))
    kspec = pl.BlockSpec((seq, LANES), lambda b, h: (b, h))
    small = lambda shape: pl.BlockSpec(shape, lambda b, h: (0, 0))
    return pl.pallas_call(
        functools.partial(_diff_attn_kernel, seq=seq),
        grid=(batch, DIFF_HEADS),
        in_specs=[pl.BlockSpec(memory_space=pltpu.SMEM),
                  tspec, kspec, tspec, small((4, HEAD_DIM)), small((8, LANES))],
        out_specs=pl.BlockSpec((seq, LANES), lambda b, h: (b, h)),
        out_shape=jax.ShapeDtypeStruct((batch * seq, DV), BF16),
        scratch_shapes=[pltpu.VMEM((seq, 2 * BLOCK), F32)] * DIFF_SCORE_BUFFERS,
        compiler_params=pltpu.CompilerParams(
            dimension_semantics=("parallel", "parallel"), vmem_limit_bytes=VMEM_LIMIT),
        name="diff_attn",
    )(bound, qd, kd, vd, lam_vec, prm)


SWA_BLOCKS_PER_STEP = 8


def _swa_kernel(q_ref, k_ref, v_ref, sink_ref, y_ref, *, seq):
    nq = seq // BLOCK
    sel0 = _head_select_mask()
    r = lax.broadcasted_iota(jnp.int32, (BLOCK, BLOCK), 0)
    c = lax.broadcasted_iota(jnp.int32, (BLOCK, BLOCK), 1)
    prev_band = jnp.where(c <= r, 0.0, NEG).astype(F32)
    next_band = jnp.where(r <= c, 0.0, NEG).astype(F32)

    rows = lambda blk: pl.ds(pl.multiple_of(blk * BLOCK, BLOCK), BLOCK)
    unroll = SWA_BLOCKS_PER_STEP if nq % SWA_BLOCKS_PER_STEP == 0 else 1

    def load_block(n):
        n_prev = jnp.maximum(n - 1, 0)
        n_next = jnp.minimum(n + 1, nq - 1)
        edge_prev = jnp.where(n >= 1, 0.0, NEG).astype(F32)
        edge_next = jnp.where(n <= nq - 2, 0.0, NEG).astype(F32)
        bias_prev = jnp.concatenate([prev_band + edge_prev] * 2, axis=1)
        bias_next = jnp.concatenate([next_band + edge_next] * 2, axis=1)
        bias = (bias_prev, bias_next)
        kb = jnp.concatenate([k_ref[rows(n_prev), :], k_ref[rows(n), :], k_ref[rows(n_next), :]],
                             axis=0)
        vb = jnp.concatenate([v_ref[0, n_prev], v_ref[0, n], v_ref[0, n_next]], axis=1)
        qms = [_pair_queries(q_ref[0, n, g * LANES:(g + 1) * LANES, :], sel0)
               for g in range(SWA_GROUP)]
        return kb, vb, bias, qms

    def step(t, carry):
        blocks = [unroll * t + u for u in range(unroll)]
        loaded = [load_block(n) for n in blocks]
        def masked_scores(kb, qm, bias):
            s = _dot(kb, qm)
            return jnp.concatenate([s[:BLOCK] + bias[0], s[BLOCK:2 * BLOCK],
                                    s[2 * BLOCK:] + bias[1]], axis=0)

        scores = [[masked_scores(kb, qm, bias) for qm in qms] for kb, _, bias, qms in loaded]
        for n, (_, vb, _, _), block_scores in zip(blocks, loaded, scores):
            outs = []
            for g, s in enumerate(block_scores):
                sink = sink_ref[:, g * 2 * BLOCK:(g + 1) * 2 * BLOCK]
                m = jnp.maximum(jnp.max(s, axis=0, keepdims=True), sink)
                e = jnp.exp2(s - m)
                den = jnp.sum(e, axis=0, keepdims=True) + jnp.exp2(sink - m)
                on = _dot(vb, e.astype(BF16)) * (1.0 / den)
                yt = jnp.concatenate([on[:HEAD_DIM, :BLOCK], on[HEAD_DIM:, BLOCK:]], axis=0)
                outs.append(yt.T.astype(BF16))
            y_ref[rows(n), :] = jnp.concatenate(outs, axis=1)
        return carry

    lax.fori_loop(0, nq // unroll, step, 0)


def _swa_attn(qs, ks, vs, sink_vec, batch, seq):
    nq = seq // BLOCK
    return pl.pallas_call(
        functools.partial(_swa_kernel, seq=seq),
        grid=(batch,),
        in_specs=[pl.BlockSpec((1, nq, SQ, BLOCK), lambda b: (b, 0, 0, 0)),
                  pl.BlockSpec((seq, SK), lambda b: (b, 0)),
                  pl.BlockSpec((1, nq, SV, BLOCK), lambda b: (b, 0, 0, 0)),
                  pl.BlockSpec((1, SWA_Q_HEADS * BLOCK), lambda b: (0, 0))],
        out_specs=pl.BlockSpec((seq, SQ), lambda b: (b, 0)),
        out_shape=jax.ShapeDtypeStruct((batch * seq, SQ), BF16),
        compiler_params=pltpu.CompilerParams(
            dimension_semantics=("parallel",), vmem_limit_bytes=VMEM_LIMIT),
        name="swa_attn",
    )(qs, ks, vs, sink_vec)


def _token_tile(n, want):
    return want if n % want == 0 else BLOCK


def kernel(x, positions, w_in, w_out, diff_lambda, diff_subln_g, swa_sink,
           ffn1_w_in, ffn1_w_out, ffn2_w_in, ffn2_w_out, ln_g, ln_b):
    batch, seq, _ = x.shape
    n = batch * seq
    tm = _token_tile(seq, 1024)
    tm_ffn = _token_tile(n, FFN_TOKEN_TILE)
    h = x.reshape(n, D_MODEL)
    cos, sin = _rope_tables(positions)

    w_in_p = _permute_in_proj(w_in).astype(BF16)
    w_out_p = _permute_out_proj(w_out).astype(BF16)
    ffn1_in, ffn1_out = ffn1_w_in.astype(BF16), ffn1_w_out.astype(BF16)
    ffn2_in, ffn2_out = ffn2_w_in.astype(BF16), ffn2_w_out.astype(BF16)
    sink = swa_sink.reshape(DEPTH, SWA_KV_HEADS, SWA_GROUP).transpose(0, 2, 1) * LOG2E
    sink_vec = jnp.repeat(sink.reshape(DEPTH, 1, SWA_Q_HEADS), BLOCK, axis=2)

    for l in range(DEPTH):
        lambda_init = 0.8 - 0.6 * math.exp(-0.3 * l)
        g = lambda i: ln_g[l, i].reshape(1, D_MODEL)
        b = lambda i: ln_b[l, i].reshape(1, D_MODEL)
        prm = jnp.zeros((8, LANES), F32)
        prm = prm.at[0].set(diff_subln_g[l]).at[1].set(1.0 - lambda_init).at[2].set(lambda_init)

        h = _ffn_ln(h, ffn1_in, ffn1_out, g(0), b(0), l, tm_ffn)
        (qd, kd, vd, qs, ks, vs), bound = _in_proj(h, w_in_p, cos, sin, l, batch, seq, tm)
        yd = _diff_attn(bound, qd, kd, vd, diff_lambda[l], prm, batch, seq)
        ys = _swa_attn(qs, ks, vs, sink_vec[l], batch, seq)
        h = _mix_ffn_ln(h, yd, ys, w_out_p, g(1), b(1), ffn2_in, ffn2_out, g(2), b(2), l, tm_ffn)
    return h.reshape(batch, seq, D_MODEL)
```

```python
import functools
import math

import jax
import jax.numpy as jnp
from jax import lax
from jax.experimental import pallas as pl
from jax.experimental.pallas import tpu as pltpu

D_MODEL = 1024
DEPTH = 4
HEAD_DIM = 64
HALF = HEAD_DIM // 2
DIFF_HEADS = 4
SWA_Q_HEADS = 8
SWA_KV_HEADS = 2
SWA_GROUP = SWA_Q_HEADS // SWA_KV_HEADS
BLOCK = 128
D_FF = 2816
ROPE_THETA = 10000.0
ALPHA = (2.0 * DEPTH) ** 0.25
LN_EPS = 1e-5
RMS_EPS = 1e-5
LOG2E = 1.4426950408889634
Q_SCALE = (HEAD_DIM ** -0.5) * LOG2E
NEG = -1e30

LANES = 128
SUBLANES = 8
DQ = DK = DV = 512
SQ = 512
SK = SV = 128
IN_COLS = DQ + DK + DV + SQ + SK + SV
ROPE_COLS = DQ + DK + SQ + SK

VMEM_LIMIT = 56 * 1024 * 1024

BF16 = jnp.bfloat16
F32 = jnp.float32


def _permute_in_proj(w):
    lead = w.shape[:-1]
    dq, dk, dv = w[..., :DQ], w[..., DQ:DQ + DK], w[..., DQ + DK:DQ + DK + DV]
    sq0 = DQ + DK + DV
    sq, sk, sv = w[..., sq0:sq0 + SQ], w[..., sq0 + SQ:sq0 + SQ + SK], w[..., sq0 + SQ + SK:]
    pair = lambda t: t.reshape(lead + (DIFF_HEADS, 2, 2, HALF)).swapaxes(-3, -2).reshape(lead + (DQ,))
    sq = jnp.moveaxis(sq.reshape(lead + (SWA_KV_HEADS, SWA_GROUP, 2, HALF)), -4, -2)
    sk = sk.reshape(lead + (SWA_KV_HEADS, 2, HALF)).swapaxes(-3, -2)
    return jnp.concatenate(
        [pair(dq), pair(dk), sq.reshape(lead + (SQ,)), sk.reshape(lead + (SK,)), dv, sv], axis=-1)


def _permute_out_proj(w):
    lead, tail = w.shape[:-2], w.shape[-1:]
    swa = w[..., DV:, :].reshape(lead + (SWA_KV_HEADS, SWA_GROUP, HEAD_DIM) + tail)
    swa = swa.swapaxes(-4, -3).reshape(lead + (SQ,) + tail)
    return jnp.concatenate([w[..., :DV, :], swa], axis=-2)


def _layer_norm(z, g, b):
    mu = jnp.mean(z, axis=-1, keepdims=True)
    zc = z - mu
    var = jnp.mean(zc * zc, axis=-1, keepdims=True)
    return zc * lax.rsqrt(var + LN_EPS) * g + b


def _dot(a, b):
    return jnp.dot(a, b, preferred_element_type=F32)


ROPE_PACK = LANES // HALF
ROPE_TILE = 2048


def _rope_table_kernel(pos_ref, inv_ref, sign_ref, cos_ref, sin_ref):
    rows = pos_ref.shape[0]
    ang = pos_ref[...] * inv_ref[...]
    cos = jnp.cos(ang)
    sin = jnp.sin(ang)
    first_group = lax.broadcasted_iota(jnp.int32, (rows, LANES), 1) < HALF

    def spread(t, j):
        x = t if j == 0 else pltpu.roll(t, LANES - HALF * j, 1)
        x = jnp.where(first_group, x, 0.0)
        x = x + pltpu.roll(x, HALF, 1)
        return x + pltpu.roll(x, 2 * HALF, 1)

    for j in range(ROPE_PACK):
        cos_ref[j * rows:(j + 1) * rows, :] = spread(cos, j)
        sin_ref[j * rows:(j + 1) * rows, :] = spread(sin, j) * sign_ref[...]


def _rope_tables(positions):
    n = positions.size
    tile = min(n, ROPE_TILE)
    rows = tile // ROPE_PACK
    inv = jnp.power(ROPE_THETA, -jnp.arange(0, HEAD_DIM, 2, dtype=F32) / HEAD_DIM)
    inv128 = jnp.tile(inv, ROPE_PACK).reshape(1, LANES)
    sign = jnp.where(jnp.arange(LANES) < LANES // 2, -1.0, 1.0).astype(F32).reshape(1, LANES)
    pos = positions.reshape(n // tile, ROPE_PACK, rows).astype(F32).transpose(0, 2, 1)
    pos = jnp.repeat(pos, HALF, axis=2).reshape(n // ROPE_PACK, LANES)
    packed = pl.BlockSpec((rows, LANES), lambda i: (i, 0))
    row = pl.BlockSpec((tile, LANES), lambda i: (i, 0))
    vec = pl.BlockSpec((1, LANES), lambda i: (0, 0))
    return pl.pallas_call(
        _rope_table_kernel,
        grid=(n // tile,),
        in_specs=[packed, vec, vec],
        out_specs=[row, row],
        out_shape=[jax.ShapeDtypeStruct((n, LANES), F32)] * 2,
        name="rope_tables",
    )(pos, inv128, sign)


FFN_CHUNK = 256


FFN_TOKEN_TILE = 1024
FFN_OUT_ROWS = 256


def _ffn_residual_ln(x, wi_ref, wo_ref, g_ref, b_ref, o_ref, act_ref, row0=0):
    xb = x.astype(BF16)
    n_rows = x.shape[0]
    for c in range(D_FF // FFN_CHUNK):
        lo = c * FFN_CHUNK
        gate = _dot(xb, wi_ref[:, lo:lo + FFN_CHUNK])
        up = _dot(xb, wi_ref[:, D_FF + lo:D_FF + lo + FFN_CHUNK])
        act = gate * (1.0 / (1.0 + jnp.exp(-gate))) * up
        act_ref[row0:row0 + n_rows, lo:lo + FFN_CHUNK] = act.astype(BF16)
    rows = min(FFN_OUT_ROWS, n_rows)
    for r in range(n_rows // rows):
        sl = slice(r * rows, (r + 1) * rows)
        out = slice(row0 + r * rows, row0 + (r + 1) * rows)
        y = _dot(act_ref[out, :], wo_ref[...])
        o_ref[out, :] = _layer_norm(ALPHA * x[sl, :] + 0.5 * y, g_ref[...], b_ref[...])


def _ffn_kernel(x_ref, wi_ref, wo_ref, g_ref, b_ref, o_ref, act_ref):
    _ffn_residual_ln(x_ref[...], wi_ref, wo_ref, g_ref, b_ref, o_ref, act_ref)


def _const_spec(shape):
    return pl.BlockSpec(shape, lambda i: (0, 0), pipeline_mode=pl.Buffered(1))


def _layer_spec(shape, layer):
    return pl.BlockSpec((None,) + shape, lambda i: (layer, 0, 0), pipeline_mode=pl.Buffered(1))


def _ffn_ln(x, w_in, w_out, g, b, layer, tm):
    n = x.shape[0]
    row = pl.BlockSpec((tm, D_MODEL), lambda i: (i, 0))
    return pl.pallas_call(
        _ffn_kernel,
        grid=(n // tm,),
        in_specs=[row, _layer_spec((D_MODEL, 2 * D_FF), layer), _layer_spec((D_FF, D_MODEL), layer),
                  _const_spec((1, D_MODEL)), _const_spec((1, D_MODEL))],
        out_specs=row,
        out_shape=jax.ShapeDtypeStruct((n, D_MODEL), F32),
        scratch_shapes=[pltpu.VMEM((tm, D_FF), BF16)],
        compiler_params=pltpu.CompilerParams(
            dimension_semantics=("parallel",), vmem_limit_bytes=VMEM_LIMIT),
        name="ffn_ln",
    )(x, w_in, w_out, g, b)


def _mix_ffn_kernel(x_ref, yd_ref, ys_ref, wm_ref, gm_ref, bm_ref,
                    wi_ref, wo_ref, g_ref, b_ref, o_ref, act_ref):
    half = x_ref.shape[0] // 2
    xs = []
    for h in range(2):
        sl = slice(h * half, (h + 1) * half)
        mix = _dot(jnp.concatenate([yd_ref[sl, :], ys_ref[sl, :]], axis=1), wm_ref[...])
        xs.append(_layer_norm(ALPHA * x_ref[sl, :] + mix, gm_ref[...], bm_ref[...]))
    for h in range(2):
        _ffn_residual_ln(xs[h], wi_ref, wo_ref, g_ref, b_ref, o_ref, act_ref, row0=h * half)


def _mix_ffn_ln(x, yd, ys, w_mix, g_mix, b_mix, w_in, w_out, g, b, layer, tm):
    n = x.shape[0]
    row = lambda width: pl.BlockSpec((tm, width), lambda i: (i, 0))
    vec = _const_spec((1, D_MODEL))
    return pl.pallas_call(
        _mix_ffn_kernel,
        grid=(n // tm,),
        in_specs=[row(D_MODEL), row(DV), row(SQ), _layer_spec((DV + SQ, D_MODEL), layer), vec, vec,
                  _layer_spec((D_MODEL, 2 * D_FF), layer), _layer_spec((D_FF, D_MODEL), layer),
                  vec, vec],
        out_specs=row(D_MODEL),
        out_shape=jax.ShapeDtypeStruct((n, D_MODEL), F32),
        scratch_shapes=[pltpu.VMEM((tm, D_FF), BF16)],
        compiler_params=pltpu.CompilerParams(
            dimension_semantics=("parallel",), vmem_limit_bytes=VMEM_LIMIT),
        name="mix_ffn_ln",
    )(x, yd, ys, w_mix, g_mix, b_mix, w_in, w_out, g, b)


def _proj_kernel(x_ref, w_ref, cos_ref, sin_ref, ones_ref,
                 qd_ref, kd_ref, vd_ref, qs_ref, ks_ref, vs_ref, norm_ref, *, tm):
    xb = x_ref[...].astype(BF16)
    cos = cos_ref[...]
    sin = sin_ref[...]

    def rope(t, scale):
        r = t * cos + pltpu.roll(t, LANES // 2, 1) * sin
        return r * scale if scale != 1.0 else r

    def put_transposed(ref, col, t):
        for blk in range(tm // BLOCK):
            ref[0, blk, col:col + LANES, :] = t[blk * BLOCK:(blk + 1) * BLOCK, :].T.astype(BF16)

    def put_rows(ref, col, t):
        ref[:, col:col + LANES] = t.astype(BF16)

    plan = []
    for j in range(DQ // LANES):
        plan.append((j * LANES, qd_ref, j * LANES, put_transposed, Q_SCALE))
    for j in range(DK // LANES):
        plan.append((DQ + j * LANES, kd_ref, j * LANES, put_rows, 1.0))
    for j in range(SQ // LANES):
        plan.append((DQ + DK + j * LANES, qs_ref, j * LANES, put_transposed, Q_SCALE))
    plan.append((DQ + DK + SQ, ks_ref, 0, put_rows, 1.0))
    for j in range(DV // LANES):
        plan.append((ROPE_COLS + j * LANES, vd_ref, j * LANES, put_transposed, None))
    plan.append((ROPE_COLS + DV, vs_ref, 0, put_transposed, None))

    squares = [None] * (2 * DIFF_HEADS)

    for first in range(0, len(plan), 2):
        lo = plan[first][0]
        t2 = _dot(xb, w_ref[:, lo:lo + 2 * LANES])
        for half, (_, ref, col, put, scale) in enumerate(plan[first:first + 2]):
            t = t2[:, half * LANES:(half + 1) * LANES]
            r = t if scale is None else rope(t, scale)
            put(ref, col, r)
            if ref is qd_ref or ref is kd_ref:
                slot = col // LANES + (0 if ref is qd_ref else DIFF_HEADS)
                squares[slot] = (r * r).astype(BF16)

    norms = _dot(jnp.concatenate(squares, axis=1), ones_ref[...])
    norm_ref[...] = jnp.broadcast_to(jnp.max(norms, axis=0, keepdims=True), (SUBLANES, LANES))[None]


def _in_proj(x, w, cos, sin, layer, batch, seq, tm):
    n = x.shape[0]
    nq = seq // BLOCK
    tpb = seq // tm
    sub = tm // BLOCK
    row = lambda width: pl.BlockSpec((tm, width), lambda i: (i, 0))
    tblk = lambda rows: pl.BlockSpec((1, sub, rows, BLOCK), lambda i: (i // tpb, i % tpb, 0, 0))
    tshape = lambda rows: jax.ShapeDtypeStruct((batch, nq, rows, BLOCK), BF16)
    group_ones = (jnp.arange(2 * DIFF_HEADS * LANES)[:, None] // LANES
                  == jnp.arange(LANES)[None, :]).astype(BF16)
    *proj, norms = pl.pallas_call(
        functools.partial(_proj_kernel, tm=tm),
        grid=(n // tm,),
        in_specs=[row(D_MODEL), _layer_spec((D_MODEL, IN_COLS), layer), row(LANES), row(LANES),
                  _const_spec((2 * DIFF_HEADS * LANES, LANES))],
        out_specs=[tblk(DQ), row(DK), tblk(DV), tblk(SQ), row(SK), tblk(SV),
                   pl.BlockSpec((1, SUBLANES, LANES), lambda i: (i, 0, 0))],
        out_shape=[tshape(DQ), jax.ShapeDtypeStruct((n, DK), BF16), tshape(DV),
                   tshape(SQ), jax.ShapeDtypeStruct((n, SK), BF16), tshape(SV),
                   jax.ShapeDtypeStruct((n // tm, SUBLANES, LANES), F32)],
        compiler_params=pltpu.CompilerParams(
            dimension_semantics=("parallel",), vmem_limit_bytes=VMEM_LIMIT),
        name="in_proj_rope",
    )(x, w, cos, sin, group_ones)
    biggest = norms[:, 0, :2 * DIFF_HEADS].reshape(batch, tpb, 2 * DIFF_HEADS).max(axis=1)
    bound = jnp.sqrt(biggest[:, :DIFF_HEADS] * biggest[:, DIFF_HEADS:]) * BF16_ROUNDING_MARGIN
    return proj, bound


def _head_select_mask():
    r = lax.broadcasted_iota(jnp.int32, (LANES, 1), 0)
    return (r % HEAD_DIM) < HALF


def _pair_queries(qt, sel0):
    zero = jnp.zeros_like(qt)
    return jnp.concatenate([jnp.where(sel0, qt, zero), jnp.where(sel0, zero, qt)], axis=1)


DIFF_KV_TILE = 256
DIFF_SCORE_BUFFERS = 4
DIRECT_BLOCKS_PER_STEP = 16
DIRECT_LOOKAHEAD = 6
DIRECT_SCORE_LIMIT = 64.0
BF16_ROUNDING_MARGIN = 1.02
DIFF_BLOCKS_PER_STEP = 4


def _diff_attn_kernel(bound_ref, q_ref, k_ref, v_ref, lam_ref, prm_ref, y_ref, *s_refs, seq):
    lv = lam_ref[...]
    a1 = jnp.sum(lv[0:1, :] * lv[1:2, :], axis=-1, keepdims=True)
    a2 = jnp.sum(lv[2:3, :] * lv[3:4, :], axis=-1, keepdims=True)
    lam = jnp.exp(a1) - jnp.exp(a2) + prm_ref[2:3, 0:1]
    gain = prm_ref[0:1, :]
    one_minus_init = prm_ref[1:2, :]
    sel0 = _head_select_mask()
    tk = min(DIFF_KV_TILE, seq)
    nq = seq // BLOCK
    tiles = [(j * tk, (j + 1) * tk) for j in range(seq // tk)]

    def scores(i, s_ref):
        qm = _pair_queries(q_ref[0, i], sel0)
        m8 = None
        for lo, hi in tiles:
            s = _dot(k_ref[lo:hi, :], qm)
            s_ref[lo:hi, :] = s
            tile_max = jnp.max(s.reshape(tk // SUBLANES, SUBLANES, 2 * BLOCK), axis=0)
            m8 = tile_max if m8 is None else jnp.maximum(m8, tile_max)
        return jnp.max(m8, axis=0, keepdims=True)

    def weighted_values(p, lo, hi, o, l8):
        l8 = l8 + jnp.sum(p.reshape(tk // SUBLANES, SUBLANES, 2 * BLOCK), axis=0)
        vt = jnp.concatenate([v_ref[0, t] for t in range(lo // BLOCK, hi // BLOCK)], axis=1)
        return o + _dot(vt, p.astype(BF16)), l8

    def finish(i, s_ref, m):
        l8 = jnp.zeros((SUBLANES, 2 * BLOCK), F32)
        o = jnp.zeros((LANES, 2 * BLOCK), F32)
        for lo, hi in tiles:
            o, l8 = weighted_values(jnp.exp2(s_ref[lo:hi, :] - m), lo, hi, o, l8)
        write_block(i, o, l8)

    def write_block(i, o, l8):
        on = o * (1.0 / jnp.sum(l8, axis=0, keepdims=True))
        d = (on[:, :BLOCK] - lam * on[:, BLOCK:]).T
        ms = jnp.mean(d * d, axis=-1, keepdims=True)
        y = d * lax.rsqrt(ms + RMS_EPS) * gain * one_minus_init
        start = i * BLOCK if isinstance(i, int) else pl.multiple_of(i * BLOCK, BLOCK)
        y_ref[pl.ds(start, BLOCK), :] = y.astype(BF16)

    nbuf = len(s_refs)
    group = DIFF_BLOCKS_PER_STEP if nq % DIFF_BLOCKS_PER_STEP == 0 else nbuf

    def block_group(t, m, last=False):
        for k in range(group):
            i = group * t + k
            if not (last and k == group - 1):
                m_next = scores(i + 1, s_refs[(k + 1) % nbuf])
            finish(i, s_refs[k % nbuf], m)
            m = m_next
        return m

    steps = nq // group

    def shifted_path():
        m = lax.fori_loop(0, steps - 1, block_group, scores(0, s_refs[0]))
        block_group(steps - 1, m, last=True)

    def direct_path():
        group = DIRECT_BLOCKS_PER_STEP if nq % DIRECT_BLOCKS_PER_STEP == 0 else nbuf
        steps = nq // group

        def direct_group(t, carry):
            blocks = [group * t + k for k in range(group)]
            qms = [None] * group
            acc = [None] * group
            pending = []

            def consume():
                k, lo, hi, s = pending.pop(0)
                o, l8 = acc[k] if acc[k] is not None else (
                    jnp.zeros((LANES, 2 * BLOCK), F32), jnp.zeros((SUBLANES, 2 * BLOCK), F32))
                acc[k] = weighted_values(jnp.exp2(s), lo, hi, o, l8)
                if hi == seq:
                    write_block(blocks[k], *acc[k])

            for k in range(group):
                qms[k] = _pair_queries(q_ref[0, blocks[k]], sel0)
                for lo, hi in tiles:
                    pending.append((k, lo, hi, _dot(k_ref[lo:hi, :], qms[k])))
                    if len(pending) > DIRECT_LOOKAHEAD:
                        consume()
            while pending:
                consume()
            return carry
        lax.fori_loop(0, steps, direct_group, 0)

    bound = bound_ref[pl.program_id(0), pl.program_id(1)]
    lax.cond(bound < DIRECT_SCORE_LIMIT, direct_path, shifted_path)


def _diff_attn(bound, qd, kd, vd, lam_vec, prm, batch, seq):
    nq = seq // BLOCK
    tspec = pl.BlockSpec((1, nq, LANES, BLOCK), lambda b, h: (b, 0, h, 0))
    kspec = pl.BlockSpec((seq, LANES), lambda b, h: (b, h))
    small = lambda shape: pl.BlockSpec(shape, lambda b, h: (0, 0))
    return pl.pallas_call(
        functools.partial(_diff_attn_kernel, seq=seq),
        grid=(batch, DIFF_HEADS),
        in_specs=[pl.BlockSpec(memory_space=pltpu.SMEM),
                  tspec, kspec, tspec, small((4, HEAD_DIM)), small((8, LANES))],
        out_specs=pl.BlockSpec((seq, LANES), lambda b, h: (b, h)),
        out_shape=jax.ShapeDtypeStruct((batch * seq, DV), BF16),
        scratch_shapes=[pltpu.VMEM((seq, 2 * BLOCK), F32)] * DIFF_SCORE_BUFFERS,
        compiler_params=pltpu.CompilerParams(
            dimension_semantics=("parallel", "parallel"), vmem_limit_bytes=VMEM_LIMIT),
        name="diff_attn",
    )(bound, qd, kd, vd, lam_vec, prm)


SWA_BLOCKS_PER_STEP = 8


def _swa_kernel(q_ref, k_ref, v_ref, sink_ref, y_ref, *, seq):
    nq = seq // BLOCK
    sel0 = _head_select_mask()
    r = lax.broadcasted_iota(jnp.int32, (BLOCK, BLOCK), 0)
    c = lax.broadcasted_iota(jnp.int32, (BLOCK, BLOCK), 1)
    prev_band = jnp.where(c <= r, 0.0, NEG).astype(F32)
    next_band = jnp.where(r <= c, 0.0, NEG).astype(F32)

    rows = lambda blk: pl.ds(pl.multiple_of(blk * BLOCK, BLOCK), BLOCK)
    unroll = SWA_BLOCKS_PER_STEP if nq % SWA_BLOCKS_PER_STEP == 0 else 1

    def load_block(n):
        n_prev = jnp.maximum(n - 1, 0)
        n_next = jnp.minimum(n + 1, nq - 1)
        edge_prev = jnp.where(n >= 1, 0.0, NEG).astype(F32)
        edge_next = jnp.where(n <= nq - 2, 0.0, NEG).astype(F32)
        bias_prev = jnp.concatenate([prev_band + edge_prev] * 2, axis=1)
        bias_next = jnp.concatenate([next_band + edge_next] * 2, axis=1)
        bias = (bias_prev, bias_next)
        kb = jnp.concatenate([k_ref[rows(n_prev), :], k_ref[rows(n), :], k_ref[rows(n_next), :]],
                             axis=0)
        vb = jnp.concatenate([v_ref[0, n_prev], v_ref[0, n], v_ref[0, n_next]], axis=1)
        qms = [_pair_queries(q_ref[0, n, g * LANES:(g + 1) * LANES, :], sel0)
               for g in range(SWA_GROUP)]
        return kb, vb, bias, qms

    def step(t, carry):
        blocks = [unroll * t + u for u in range(unroll)]
        loaded = [load_block(n) for n in blocks]
        def masked_scores(kb, qm, bias):
            s = _dot(kb, qm)
            return jnp.concatenate([s[:BLOCK] + bias[0], s[BLOCK:2 * BLOCK],
                                    s[2 * BLOCK:] + bias[1]], axis=0)

        scores = [[masked_scores(kb, qm, bias) for qm in qms] for kb, _, bias, qms in loaded]
        for n, (_, vb, _, _), block_scores in zip(blocks, loaded, scores):
            outs = []
            for g, s in enumerate(block_scores):
                sink = sink_ref[:, g * 2 * BLOCK:(g + 1) * 2 * BLOCK]
                m = jnp.maximum(jnp.max(s, axis=0, keepdims=True), sink)
                e = jnp.exp2(s - m)
                den = jnp.sum(e, axis=0, keepdims=True) + jnp.exp2(sink - m)
                on = _dot(vb, e.astype(BF16)) * (1.0 / den)
                yt = jnp.concatenate([on[:HEAD_DIM, :BLOCK], on[HEAD_DIM:, BLOCK:]], axis=0)
                outs.append(yt.T.astype(BF16))
            y_ref[rows(n), :] = jnp.concatenate(outs, axis=1)
        return carry

    lax.fori_loop(0, nq // unroll, step, 0)


def _swa_attn(qs, ks, vs, sink_vec, batch, seq):
    nq = seq // BLOCK
    return pl.pallas_call(
        functools.partial(_swa_kernel, seq=seq),
        grid=(batch,),
        in_specs=[pl.BlockSpec((1, nq, SQ, BLOCK), lambda b: (b, 0, 0, 0)),
                  pl.BlockSpec((seq, SK), lambda b: (b, 0)),
                  pl.BlockSpec((1, nq, SV, BLOCK), lambda b: (b, 0, 0, 0)),
                  pl.BlockSpec((1, SWA_Q_HEADS * BLOCK), lambda b: (0, 0))],
        out_specs=pl.BlockSpec((seq, SQ), lambda b: (b, 0)),
        out_shape=jax.ShapeDtypeStruct((batch * seq, SQ), BF16),
        compiler_params=pltpu.CompilerParams(
            dimension_semantics=("parallel",), vmem_limit_bytes=VMEM_LIMIT),
        name="swa_attn",
    )(qs, ks, vs, sink_vec)


def _token_tile(n, want):
    return want if n % want == 0 else BLOCK


def kernel(x, positions, w_in, w_out, diff_lambda, diff_subln_g, swa_sink,
           ffn1_w_in, ffn1_w_out, ffn2_w_in, ffn2_w_out, ln_g, ln_b):
    batch, seq, _ = x.shape
    n = batch * seq
    tm = _token_tile(seq, 1024)
    tm_ffn = _token_tile(n, FFN_TOKEN_TILE)
    h = x.reshape(n, D_MODEL)
    cos, sin = _rope_tables(positions)

    w_in_p = _permute_in_proj(w_in).astype(BF16)
    w_out_p = _permute_out_proj(w_out).astype(BF16)
    ffn1_in, ffn1_out = ffn1_w_in.astype(BF16), ffn1_w_out.astype(BF16)
    ffn2_in, ffn2_out = ffn2_w_in.astype(BF16), ffn2_w_out.astype(BF16)
    sink = swa_sink.reshape(DEPTH, SWA_KV_HEADS, SWA_GROUP).transpose(0, 2, 1) * LOG2E
    sink_vec = jnp.repeat(sink.reshape(DEPTH, 1, SWA_Q_HEADS), BLOCK, axis=2)

    for l in range(DEPTH):
        lambda_init = 0.8 - 0.6 * math.exp(-0.3 * l)
        g = lambda i: ln_g[l, i].reshape(1, D_MODEL)
        b = lambda i: ln_b[l, i].reshape(1, D_MODEL)
        prm = jnp.zeros((8, LANES), F32)
        prm = prm.at[0].set(diff_subln_g[l]).at[1].set(1.0 - lambda_init).at[2].set(lambda_init)

        h = _ffn_ln(h, ffn1_in, ffn1_out, g(0), b(0), l, tm_ffn)
        (qd, kd, vd, qs, ks, vs), bound = _in_proj(h, w_in_p, cos, sin, l, batch, seq, tm)
        yd = _diff_attn(bound, qd, kd, vd, diff_lambda[l], prm, batch, seq)
        ys = _swa_attn(qs, ks, vs, sink_vec[l], batch, seq)
        h = _mix_ffn_ln(h, yd, ys, w_out_p, g(1), b(1), ffn2_in, ffn2_out, g(2), b(2), l, tm_ffn)
    return h.reshape(batch, seq, D_MODEL)
```

```python
import functools
import math

import jax
import jax.numpy as jnp
from jax import lax
from jax.experimental import pallas as pl
from jax.experimental.pallas import tpu as pltpu

D_MODEL = 1024
DEPTH = 4
HEAD_DIM = 64
HALF = HEAD_DIM // 2
DIFF_HEADS = 4
SWA_Q_HEADS = 8
SWA_KV_HEADS = 2
SWA_GROUP = SWA_Q_HEADS // SWA_KV_HEADS
BLOCK = 128
D_FF = 2816
ROPE_THETA = 10000.0
ALPHA = (2.0 * DEPTH) ** 0.25
LN_EPS = 1e-5
RMS_EPS = 1e-5
LOG2E = 1.4426950408889634
Q_SCALE = (HEAD_DIM ** -0.5) * LOG2E
NEG = -1e30

LANES = 128
SUBLANES = 8
DQ = DK = DV = 512
SQ = 512
SK = SV = 128
IN_COLS = DQ + DK + DV + SQ + SK + SV
ROPE_COLS = DQ + DK + SQ + SK

VMEM_LIMIT = 56 * 1024 * 1024

NORM_GROUPS = 2 * DIFF_HEADS + SWA_GROUP + 1
DIRECT_SCORE_LIMIT = 64.0
BF16_ROUNDING_MARGIN = 1.02

BF16 = jnp.bfloat16
F32 = jnp.float32


def _permute_in_proj(w):
    lead = w.shape[:-1]
    dq, dk, dv = w[..., :DQ], w[..., DQ:DQ + DK], w[..., DQ + DK:DQ + DK + DV]
    sq0 = DQ + DK + DV
    sq, sk, sv = w[..., sq0:sq0 + SQ], w[..., sq0 + SQ:sq0 + SQ + SK], w[..., sq0 + SQ + SK:]
    pair = lambda t: t.reshape(lead + (DIFF_HEADS, 2, 2, HALF)).swapaxes(-3, -2).reshape(lead + (DQ,))
    sq = jnp.moveaxis(sq.reshape(lead + (SWA_KV_HEADS, SWA_GROUP, 2, HALF)), -4, -2)
    sk = sk.reshape(lead + (SWA_KV_HEADS, 2, HALF)).swapaxes(-3, -2)
    return jnp.concatenate(
        [pair(dq), pair(dk), sq.reshape(lead + (SQ,)), sk.reshape(lead + (SK,)), dv, sv], axis=-1)


def _permute_out_proj(w):
    lead, tail = w.shape[:-2], w.shape[-1:]
    swa = w[..., DV:, :].reshape(lead + (SWA_KV_HEADS, SWA_GROUP, HEAD_DIM) + tail)
    swa = swa.swapaxes(-4, -3).reshape(lead + (SQ,) + tail)
    return jnp.concatenate([w[..., :DV, :], swa], axis=-2)


def _layer_norm(z, g, b):
    mu = jnp.mean(z, axis=-1, keepdims=True)
    zc = z - mu
    var = jnp.mean(zc * zc, axis=-1, keepdims=True)
    return zc * lax.rsqrt(var + LN_EPS) * g + b


def _dot(a, b):
    return jnp.dot(a, b, preferred_element_type=F32)


ROPE_PACK = LANES // HALF
ROPE_TILE = 2048


def _rope_table_kernel(pos_ref, inv_ref, sign_ref, cos_ref, sin_ref):
    rows = pos_ref.shape[0]
    ang = pos_ref[...] * inv_ref[...]
    cos = jnp.cos(ang)
    sin = jnp.sin(ang)
    first_group = lax.broadcasted_iota(jnp.int32, (rows, LANES), 1) < HALF

    def spread(t, j):
        x = t if j == 0 else pltpu.roll(t, LANES - HALF * j, 1)
        x = jnp.where(first_group, x, 0.0)
        x = x + pltpu.roll(x, HALF, 1)
        return x + pltpu.roll(x, 2 * HALF, 1)

    for j in range(ROPE_PACK):
        cos_ref[j * rows:(j + 1) * rows, :] = spread(cos, j)
        sin_ref[j * rows:(j + 1) * rows, :] = spread(sin, j) * sign_ref[...]


def _rope_tables(positions):
    n = positions.size
    tile = min(n, ROPE_TILE)
    rows = tile // ROPE_PACK
    inv = jnp.power(ROPE_THETA, -jnp.arange(0, HEAD_DIM, 2, dtype=F32) / HEAD_DIM)
    inv128 = jnp.tile(inv, ROPE_PACK).reshape(1, LANES)
    sign = jnp.where(jnp.arange(LANES) < LANES // 2, -1.0, 1.0).astype(F32).reshape(1, LANES)
    pos = positions.reshape(n // tile, ROPE_PACK, rows).astype(F32).transpose(0, 2, 1)
    pos = jnp.repeat(pos, HALF, axis=2).reshape(n // ROPE_PACK, LANES)
    packed = pl.BlockSpec((rows, LANES), lambda i: (i, 0))
    row = pl.BlockSpec((tile, LANES), lambda i: (i, 0))
    vec = pl.BlockSpec((1, LANES), lambda i: (0, 0))
    return pl.pallas_call(
        _rope_table_kernel,
        grid=(n // tile,),
        in_specs=[packed, vec, vec],
        out_specs=[row, row],
        out_shape=[jax.ShapeDtypeStruct((n, LANES), F32)] * 2,
        name="rope_tables",
    )(pos, inv128, sign)


FFN_CHUNK = 256
FFN_TOKEN_TILE = 1024
FFN_OUT_ROWS = 256


def _ffn_residual_ln(x, wi_ref, wo_ref, g_ref, b_ref, o_ref, act_ref, row0=0):
    xb = x.astype(BF16)
    n_rows = x.shape[0]
    for c in range(D_FF // FFN_CHUNK):
        lo = c * FFN_CHUNK
        gate = _dot(xb, wi_ref[:, lo:lo + FFN_CHUNK])
        up = _dot(xb, wi_ref[:, D_FF + lo:D_FF + lo + FFN_CHUNK])
        act = gate * (1.0 / (1.0 + jnp.exp(-gate))) * up
        act_ref[row0:row0 + n_rows, lo:lo + FFN_CHUNK] = act.astype(BF16)
    rows = min(FFN_OUT_ROWS, n_rows)
    for r in range(n_rows // rows):
        sl = slice(r * rows, (r + 1) * rows)
        out = slice(row0 + r * rows, row0 + (r + 1) * rows)
        y = _dot(act_ref[out, :], wo_ref[...])
        o_ref[out, :] = _layer_norm(ALPHA * x[sl, :] + 0.5 * y, g_ref[...], b_ref[...])


def _ffn_kernel(x_ref, wi_ref, wo_ref, g_ref, b_ref, o_ref, act_ref):
    _ffn_residual_ln(x_ref[...], wi_ref, wo_ref, g_ref, b_ref, o_ref, act_ref)


def _const_spec(shape):
    return pl.BlockSpec(shape, lambda i: (0, 0), pipeline_mode=pl.Buffered(1))


def _layer_spec(shape, layer):
    return pl.BlockSpec((None,) + shape, lambda i: (layer, 0, 0), pipeline_mode=pl.Buffered(1))


def _ffn_ln(x, w_in, w_out, g, b, layer, tm):
    n = x.shape[0]
    row = pl.BlockSpec((tm, D_MODEL), lambda i: (i, 0))
    return pl.pallas_call(
        _ffn_kernel,
        grid=(n // tm,),
        in_specs=[row, _layer_spec((D_MODEL, 2 * D_FF), layer), _layer_spec((D_FF, D_MODEL), layer),
                  _const_spec((1, D_MODEL)), _const_spec((1, D_MODEL))],
        out_specs=row,
        out_shape=jax.ShapeDtypeStruct((n, D_MODEL), F32),
        scratch_shapes=[pltpu.VMEM((tm, D_FF), BF16)],
        compiler_params=pltpu.CompilerParams(
            dimension_semantics=("parallel",), vmem_limit_bytes=VMEM_LIMIT),
        name="ffn_ln",
    )(x, w_in, w_out, g, b)


def _mix_ffn_kernel(x_ref, yd_ref, ys_ref, wm_ref, gm_ref, bm_ref,
                    wi_ref, wo_ref, g_ref, b_ref, o_ref, act_ref):
    half = x_ref.shape[0] // 2
    xs = []
    for h in range(2):
        sl = slice(h * half, (h + 1) * half)
        mix = _dot(jnp.concatenate([yd_ref[sl, :], ys_ref[sl, :]], axis=1), wm_ref[...])
        xs.append(_layer_norm(ALPHA * x_ref[sl, :] + mix, gm_ref[...], bm_ref[...]))
    for h in range(2):
        _ffn_residual_ln(xs[h], wi_ref, wo_ref, g_ref, b_ref, o_ref, act_ref, row0=h * half)


def _mix_ffn_ln(x, yd, ys, w_mix, g_mix, b_mix, w_in, w_out, g, b, layer, tm):
    n = x.shape[0]
    row = lambda width: pl.BlockSpec((tm, width), lambda i: (i, 0))
    vec = _const_spec((1, D_MODEL))
    return pl.pallas_call(
        _mix_ffn_kernel,
        grid=(n // tm,),
        in_specs=[row(D_MODEL), row(DV), row(SQ), _layer_spec((DV + SQ, D_MODEL), layer), vec, vec,
                  _layer_spec((D_MODEL, 2 * D_FF), layer), _layer_spec((D_FF, D_MODEL), layer),
                  vec, vec],
        out_specs=row(D_MODEL),
        out_shape=jax.ShapeDtypeStruct((n, D_MODEL), F32),
        scratch_shapes=[pltpu.VMEM((tm, D_FF), BF16)],
        compiler_params=pltpu.CompilerParams(
            dimension_semantics=("parallel",), vmem_limit_bytes=VMEM_LIMIT),
        name="mix_ffn_ln",
    )(x, yd, ys, w_mix, g_mix, b_mix, w_in, w_out, g, b)


def _proj_kernel(x_ref, w_ref, cos_ref, sin_ref, ones_ref,
                 qd_ref, kd_ref, vd_ref, qs_ref, ks_ref, vs_ref, norm_ref, *, tm):
    xb = x_ref[...].astype(BF16)
    cos = cos_ref[...]
    sin = sin_ref[...]

    def rope(t, scale):
        r = t * cos + pltpu.roll(t, LANES // 2, 1) * sin
        return r * scale if scale != 1.0 else r

    def put_transposed(ref, col, t):
        for blk in range(tm // BLOCK):
            ref[0, blk, col:col + LANES, :] = t[blk * BLOCK:(blk + 1) * BLOCK, :].T.astype(BF16)

    def put_rows(ref, col, t):
        ref[:, col:col + LANES] = t.astype(BF16)

    plan = []
    for j in range(DQ // LANES):
        plan.append((j * LANES, qd_ref, j * LANES, put_transposed, Q_SCALE))
    for j in range(DK // LANES):
        plan.append((DQ + j * LANES, kd_ref, j * LANES, put_rows, 1.0))
    for j in range(SQ // LANES):
        plan.append((DQ + DK + j * LANES, qs_ref, j * LANES, put_transposed, Q_SCALE))
    plan.append((DQ + DK + SQ, ks_ref, 0, put_rows, 1.0))
    for j in range(DV // LANES):
        plan.append((ROPE_COLS + j * LANES, vd_ref, j * LANES, put_transposed, None))
    plan.append((ROPE_COLS + DV, vs_ref, 0, put_transposed, None))

    squares = [None] * NORM_GROUPS
    norm_base = ((qd_ref, 0), (kd_ref, DIFF_HEADS), (qs_ref, 2 * DIFF_HEADS),
                 (ks_ref, 2 * DIFF_HEADS + SWA_GROUP))

    for first in range(0, len(plan), 2):
        lo = plan[first][0]
        t2 = _dot(xb, w_ref[:, lo:lo + 2 * LANES])
        for half, (_, ref, col, put, scale) in enumerate(plan[first:first + 2]):
            t = t2[:, half * LANES:(half + 1) * LANES]
            r = t if scale is None else rope(t, scale)
            put(ref, col, r)
            for normed_ref, base in norm_base:
                if ref is normed_ref:
                    squares[base + col // LANES] = (r * r).astype(BF16)

    norms = _dot(jnp.concatenate(squares, axis=1), ones_ref[...])
    norm_ref[...] = jnp.broadcast_to(jnp.max(norms, axis=0, keepdims=True), (SUBLANES, LANES))[None]


def _in_proj(x, w, cos, sin, layer, batch, seq, tm):
    n = x.shape[0]
    nq = seq // BLOCK
    tpb = seq // tm
    sub = tm // BLOCK
    row = lambda width: pl.BlockSpec((tm, width), lambda i: (i, 0))
    tblk = lambda rows: pl.BlockSpec((1, sub, rows, BLOCK), lambda i: (i // tpb, i % tpb, 0, 0))
    tshape = lambda rows: jax.ShapeDtypeStruct((batch, nq, rows, BLOCK), BF16)
    group_ones = (jnp.arange(NORM_GROUPS * LANES)[:, None] // LANES
                  == jnp.arange(LANES)[None, :]).astype(BF16)
    *proj, norms = pl.pallas_call(
        functools.partial(_proj_kernel, tm=tm),
        grid=(n // tm,),
        in_specs=[row(D_MODEL), _layer_spec((D_MODEL, IN_COLS), layer), row(LANES), row(LANES),
                  _const_spec((NORM_GROUPS * LANES, LANES))],
        out_specs=[tblk(DQ), row(DK), tblk(DV), tblk(SQ), row(SK), tblk(SV),
                   pl.BlockSpec((1, SUBLANES, LANES), lambda i: (i, 0, 0))],
        out_shape=[tshape(DQ), jax.ShapeDtypeStruct((n, DK), BF16), tshape(DV),
                   tshape(SQ), jax.ShapeDtypeStruct((n, SK), BF16), tshape(SV),
                   jax.ShapeDtypeStruct((n // tm, SUBLANES, LANES), F32)],
        compiler_params=pltpu.CompilerParams(
            dimension_semantics=("parallel",), vmem_limit_bytes=VMEM_LIMIT),
        name="in_proj_rope",
    )(x, w, cos, sin, group_ones)
    biggest = norms[:, 0, :NORM_GROUPS].reshape(batch, tpb, NORM_GROUPS).max(axis=1)
    q2, k2 = biggest[:, :DIFF_HEADS], biggest[:, DIFF_HEADS:2 * DIFF_HEADS]
    sq2, sk2 = biggest[:, 2 * DIFF_HEADS:NORM_GROUPS - 1].max(axis=1), biggest[:, NORM_GROUPS - 1]
    diff_bound = jnp.sqrt(q2 * k2) * BF16_ROUNDING_MARGIN
    swa_bound = jnp.sqrt(sq2 * sk2) * BF16_ROUNDING_MARGIN
    return proj, diff_bound, swa_bound


def _head_select_mask():
    r = lax.broadcasted_iota(jnp.int32, (LANES, 1), 0)
    return (r % HEAD_DIM) < HALF


def _pair_queries(qt, sel0):
    zero = jnp.zeros_like(qt)
    return jnp.concatenate([jnp.where(sel0, qt, zero), jnp.where(sel0, zero, qt)], axis=1)


DIFF_KV_TILE = 256
DIFF_SCORE_BUFFERS = 4
DIFF_BLOCKS_PER_STEP = 4
DIRECT_BLOCKS_PER_STEP = 16
DIRECT_LOOKAHEAD = 6


def _diff_attn_kernel(bound_ref, q_ref, k_ref, v_ref, lam_ref, prm_ref, y_ref, *s_refs, seq):
    lv = lam_ref[...]
    a1 = jnp.sum(lv[0:1, :] * lv[1:2, :], axis=-1, keepdims=True)
    a2 = jnp.sum(lv[2:3, :] * lv[3:4, :], axis=-1, keepdims=True)
    lam = jnp.exp(a1) - jnp.exp(a2) + prm_ref[2:3, 0:1]
    gain = prm_ref[0:1, :]
    one_minus_init = prm_ref[1:2, :]
    sel0 = _head_select_mask()
    tk = min(DIFF_KV_TILE, seq)
    nq = seq // BLOCK
    tiles = [(j * tk, (j + 1) * tk) for j in range(seq // tk)]

    def scores(i, s_ref):
        qm = _pair_queries(q_ref[0, i], sel0)
        m8 = None
        for lo, hi in tiles:
            s = _dot(k_ref[lo:hi, :], qm)
            s_ref[lo:hi, :] = s
            tile_max = jnp.max(s.reshape(tk // SUBLANES, SUBLANES, 2 * BLOCK), axis=0)
            m8 = tile_max if m8 is None else jnp.maximum(m8, tile_max)
        return jnp.max(m8, axis=0, keepdims=True)

    def weighted_values(p, lo, hi, o, l8):
        l8 = l8 + jnp.sum(p.reshape(tk // SUBLANES, SUBLANES, 2 * BLOCK), axis=0)
        vt = jnp.concatenate([v_ref[0, t] for t in range(lo // BLOCK, hi // BLOCK)], axis=1)
        return o + _dot(vt, p.astype(BF16)), l8

    def finish(i, s_ref, m):
        l8 = jnp.zeros((SUBLANES, 2 * BLOCK), F32)
        o = jnp.zeros((LANES, 2 * BLOCK), F32)
        for lo, hi in tiles:
            o, l8 = weighted_values(jnp.exp2(s_ref[lo:hi, :] - m), lo, hi, o, l8)
        write_block(i, o, l8)

    def write_block(i, o, l8):
        on = o * (1.0 / jnp.sum(l8, axis=0, keepdims=True))
        d = (on[:, :BLOCK] - lam * on[:, BLOCK:]).T
        ms = jnp.mean(d * d, axis=-1, keepdims=True)
        y = d * lax.rsqrt(ms + RMS_EPS) * gain * one_minus_init
        start = i * BLOCK if isinstance(i, int) else pl.multiple_of(i * BLOCK, BLOCK)
        y_ref[pl.ds(start, BLOCK), :] = y.astype(BF16)

    nbuf = len(s_refs)

    def shifted_path():
        group = DIFF_BLOCKS_PER_STEP if nq % DIFF_BLOCKS_PER_STEP == 0 else nbuf
        steps = nq // group

        def block_group(t, m, last=False):
            for k in range(group):
                i = group * t + k
                if not (last and k == group - 1):
                    m_next = scores(i + 1, s_refs[(k + 1) % nbuf])
                finish(i, s_refs[k % nbuf], m)
                m = m_next
            return m

        m = lax.fori_loop(0, steps - 1, block_group, scores(0, s_refs[0]))
        block_group(steps - 1, m, last=True)

    def direct_path():
        group = DIRECT_BLOCKS_PER_STEP if nq % DIRECT_BLOCKS_PER_STEP == 0 else nbuf
        steps = nq // group

        def direct_group(t, carry):
            blocks = [group * t + k for k in range(group)]
            qms = [None] * group
            acc = [None] * group
            pending = []

            def consume():
                k, lo, hi, s = pending.pop(0)
                o, l8 = acc[k] if acc[k] is not None else (
                    jnp.zeros((LANES, 2 * BLOCK), F32), jnp.zeros((SUBLANES, 2 * BLOCK), F32))
                acc[k] = weighted_values(jnp.exp2(s), lo, hi, o, l8)
                if hi == seq:
                    write_block(blocks[k], *acc[k])

            for k in range(group):
                qms[k] = _pair_queries(q_ref[0, blocks[k]], sel0)
                for lo, hi in tiles:
                    pending.append((k, lo, hi, _dot(k_ref[lo:hi, :], qms[k])))
                    if len(pending) > DIRECT_LOOKAHEAD:
                        consume()
            while pending:
                consume()
            return carry
        lax.fori_loop(0, steps, direct_group, 0)

    bound = bound_ref[pl.program_id(0), pl.program_id(1)]
    lax.cond(bound < DIRECT_SCORE_LIMIT, direct_path, shifted_path)


def _diff_attn(bound, qd, kd, vd, lam_vec, prm, batch, seq):
    nq = seq // BLOCK
    tspec = pl.BlockSpec((1, nq, LANES, BLOCK), lambda b, h: (b, 0, h, 0))
    kspec = pl.BlockSpec((seq, LANES), lambda b, h: (b, h))
    small = lambda shape: pl.BlockSpec(shape, lambda b, h: (0, 0))
    return pl.pallas_call(
        functools.partial(_diff_attn_kernel, seq=seq),
        grid=(batch, DIFF_HEADS),
        in_specs=[pl.BlockSpec(memory_space=pltpu.SMEM),
                  tspec, kspec, tspec, small((4, HEAD_DIM)), small((8, LANES))],
        out_specs=pl.BlockSpec((seq, LANES), lambda b, h: (b, h)),
        out_shape=jax.ShapeDtypeStruct((batch * seq, DV), BF16),
        scratch_shapes=[pltpu.VMEM((seq, 2 * BLOCK), F32)] * DIFF_SCORE_BUFFERS,
        compiler_params=pltpu.CompilerParams(
            dimension_semantics=("parallel", "parallel"), vmem_limit_bytes=VMEM_LIMIT),
        name="diff_attn",
    )(bound, qd, kd, vd, lam_vec, prm)


SWA_BLOCKS_PER_STEP = 8
SWA_LOOKAHEAD = 6


def _swa_kernel(bound_ref, q_ref, k_ref, v_ref, sink_ref, y_ref, *, seq):
    nq = seq // BLOCK
    sel0 = _head_select_mask()
    r = lax.broadcasted_iota(jnp.int32, (BLOCK, BLOCK), 0)
    c = lax.broadcasted_iota(jnp.int32, (BLOCK, BLOCK), 1)
    prev_band = jnp.where(c <= r, 0.0, NEG).astype(F32)
    next_band = jnp.where(r <= c, 0.0, NEG).astype(F32)

    rows = lambda blk: pl.ds(pl.multiple_of(blk * BLOCK, BLOCK), BLOCK)
    unroll = SWA_BLOCKS_PER_STEP if nq % SWA_BLOCKS_PER_STEP == 0 else 1

    def load_block(n):
        n_prev = jnp.maximum(n - 1, 0)
        n_next = jnp.minimum(n + 1, nq - 1)
        edge_prev = jnp.where(n >= 1, 0.0, NEG).astype(F32)
        edge_next = jnp.where(n <= nq - 2, 0.0, NEG).astype(F32)
        bias_prev = jnp.concatenate([prev_band + edge_prev] * 2, axis=1)
        bias_next = jnp.concatenate([next_band + edge_next] * 2, axis=1)
        bias = (bias_prev, bias_next)
        kb = jnp.concatenate([k_ref[rows(n_prev), :], k_ref[rows(n), :], k_ref[rows(n_next), :]],
                             axis=0)
        vb = jnp.concatenate([v_ref[0, n_prev], v_ref[0, n], v_ref[0, n_next]], axis=1)
        qms = [_pair_queries(q_ref[0, n, g * LANES:(g + 1) * LANES, :], sel0)
               for g in range(SWA_GROUP)]
        return kb, vb, bias, qms

    def masked_scores(kb, qm, bias):
        s = _dot(kb, qm)
        return jnp.concatenate([s[:BLOCK] + bias[0], s[BLOCK:2 * BLOCK],
                                s[2 * BLOCK:] + bias[1]], axis=0)

    def step(t, carry, shifted):
        blocks = [unroll * t + u for u in range(unroll)]
        loaded = [load_block(n) for n in blocks]
        pending, outs = [], [[] for _ in blocks]

        def consume():
            u, vb, g, s = pending.pop(0)
            sink = sink_ref[:, g * 2 * BLOCK:(g + 1) * 2 * BLOCK]
            if shifted:
                m = jnp.maximum(jnp.max(s, axis=0, keepdims=True), sink)
                e = jnp.exp2(s - m)
                den = jnp.sum(e, axis=0, keepdims=True) + jnp.exp2(sink - m)
            else:
                e = jnp.exp2(s)
                den = jnp.sum(e, axis=0, keepdims=True) + jnp.exp2(sink)
            on = _dot(vb, e.astype(BF16)) * (1.0 / den)
            yt = jnp.concatenate([on[:HEAD_DIM, :BLOCK], on[HEAD_DIM:, BLOCK:]], axis=0)
            outs[u].append(yt.T.astype(BF16))
            if g == SWA_GROUP - 1:
                y_ref[rows(blocks[u]), :] = jnp.concatenate(outs[u], axis=1)

        for u, (kb, vb, bias, qms) in enumerate(loaded):
            for g, qm in enumerate(qms):
                pending.append((u, vb, g, masked_scores(kb, qm, bias)))
                if len(pending) > SWA_LOOKAHEAD:
                    consume()
        while pending:
            consume()
        return carry

    lax.cond(bound_ref[pl.program_id(0)] < DIRECT_SCORE_LIMIT,
             lambda: lax.fori_loop(0, nq // unroll, functools.partial(step, shifted=False), 0),
             lambda: lax.fori_loop(0, nq // unroll, functools.partial(step, shifted=True), 0))


def _swa_attn(bound, qs, ks, vs, sink_vec, batch, seq):
    nq = seq // BLOCK
    return pl.pallas_call(
        functools.partial(_swa_kernel, seq=seq),
        grid=(batch,),
        in_specs=[pl.BlockSpec(memory_space=pltpu.SMEM),
                  pl.BlockSpec((1, nq, SQ, BLOCK), lambda b: (b, 0, 0, 0)),
                  pl.BlockSpec((seq, SK), lambda b: (b, 0)),
                  pl.BlockSpec((1, nq, SV, BLOCK), lambda b: (b, 0, 0, 0)),
                  pl.BlockSpec((1, SWA_Q_HEADS * BLOCK), lambda b: (0, 0))],
        out_specs=pl.BlockSpec((seq, SQ), lambda b: (b, 0)),
        out_shape=jax.ShapeDtypeStruct((batch * seq, SQ), BF16),
        compiler_params=pltpu.CompilerParams(
            dimension_semantics=("parallel",), vmem_limit_bytes=VMEM_LIMIT),
        name="swa_attn",
    )(bound, qs, ks, vs, sink_vec)


def _token_tile(n, want):
    return want if n % want == 0 else BLOCK


def kernel(x, positions, w_in, w_out, diff_lambda, diff_subln_g, swa_sink,
           ffn1_w_in, ffn1_w_out, ffn2_w_in, ffn2_w_out, ln_g, ln_b):
    batch, seq, _ = x.shape
    n = batch * seq
    tm = _token_tile(seq, 1024)
    tm_ffn = _token_tile(n, FFN_TOKEN_TILE)
    h = x.reshape(n, D_MODEL)
    cos, sin = _rope_tables(positions)

    w_in_p = _permute_in_proj(w_in).astype(BF16)
    w_out_p = _permute_out_proj(w_out).astype(BF16)
    ffn1_in, ffn1_out = ffn1_w_in.astype(BF16), ffn1_w_out.astype(BF16)
    ffn2_in, ffn2_out = ffn2_w_in.astype(BF16), ffn2_w_out.astype(BF16)
    sink = swa_sink.reshape(DEPTH, SWA_KV_HEADS, SWA_GROUP).transpose(0, 2, 1) * LOG2E
    sink_vec = jnp.repeat(sink.reshape(DEPTH, 1, SWA_Q_HEADS), BLOCK, axis=2)

    for l in range(DEPTH):
        lambda_init = 0.8 - 0.6 * math.exp(-0.3 * l)
        g = lambda i: ln_g[l, i].reshape(1, D_MODEL)
        b = lambda i: ln_b[l, i].reshape(1, D_MODEL)
        prm = jnp.zeros((8, LANES), F32)
        prm = prm.at[0].set(diff_subln_g[l]).at[1].set(1.0 - lambda_init).at[2].set(lambda_init)

        h = _ffn_ln(h, ffn1_in, ffn1_out, g(0), b(0), l, tm_ffn)
        (qd, kd, vd, qs, ks, vs), diff_bound, swa_bound = _in_proj(
            h, w_in_p, cos, sin, l, batch, seq, tm)
        swa_bound = jnp.maximum(swa_bound, jnp.max(jnp.abs(sink_vec[l])))
        yd = _diff_attn(diff_bound, qd, kd, vd, diff_lambda[l], prm, batch, seq)
        ys = _swa_attn(swa_bound, qs, ks, vs, sink_vec[l], batch, seq)
        h = _mix_ffn_ln(h, yd, ys, w_out_p, g(1), b(1), ffn2_in, ffn2_out, g(2), b(2), l, tm_ffn)
    return h.reshape(batch, seq, D_MODEL)
```

```python
import functools
import math

import jax
import jax.numpy as jnp
from jax import lax
from jax.experimental import pallas as pl
from jax.experimental.pallas import tpu as pltpu

D_MODEL = 1024
DEPTH = 4
HEAD_DIM = 64
HALF = HEAD_DIM // 2
DIFF_HEADS = 4
SWA_Q_HEADS = 8
SWA_KV_HEADS = 2
SWA_GROUP = SWA_Q_HEADS // SWA_KV_HEADS
BLOCK = 128
D_FF = 2816
ROPE_THETA = 10000.0
ALPHA = (2.0 * DEPTH) ** 0.25
LN_EPS = 1e-5
RMS_EPS = 1e-5
LOG2E = 1.4426950408889634
Q_SCALE = (HEAD_DIM ** -0.5) * LOG2E
NEG = -1e30

LANES = 128
SUBLANES = 8
DQ = DK = DV = 512
SQ = 512
SK = SV = 128
IN_COLS = DQ + DK + DV + SQ + SK + SV
ROPE_COLS = DQ + DK + SQ + SK

VMEM_LIMIT = 56 * 1024 * 1024

NORM_GROUPS = 2 * DIFF_HEADS + SWA_GROUP + 1
DIRECT_SCORE_LIMIT = 64.0
BF16_ROUNDING_MARGIN = 1.02

BF16 = jnp.bfloat16
F32 = jnp.float32


def _permute_in_proj(w):
    lead = w.shape[:-1]
    dq, dk, dv = w[..., :DQ], w[..., DQ:DQ + DK], w[..., DQ + DK:DQ + DK + DV]
    sq0 = DQ + DK + DV
    sq, sk, sv = w[..., sq0:sq0 + SQ], w[..., sq0 + SQ:sq0 + SQ + SK], w[..., sq0 + SQ + SK:]
    pair = lambda t: t.reshape(lead + (DIFF_HEADS, 2, 2, HALF)).swapaxes(-3, -2).reshape(lead + (DQ,))
    sq = jnp.moveaxis(sq.reshape(lead + (SWA_KV_HEADS, SWA_GROUP, 2, HALF)), -4, -2)
    sk = sk.reshape(lead + (SWA_KV_HEADS, 2, HALF)).swapaxes(-3, -2)
    return jnp.concatenate(
        [pair(dq), pair(dk), sq.reshape(lead + (SQ,)), sk.reshape(lead + (SK,)), dv, sv], axis=-1)


def _permute_out_proj(w):
    lead, tail = w.shape[:-2], w.shape[-1:]
    swa = w[..., DV:, :].reshape(lead + (SWA_KV_HEADS, SWA_GROUP, HEAD_DIM) + tail)
    swa = swa.swapaxes(-4, -3).reshape(lead + (SQ,) + tail)
    return jnp.concatenate([w[..., :DV, :], swa], axis=-2)


def _layer_norm(z, g, b):
    mu = jnp.mean(z, axis=-1, keepdims=True)
    zc = z - mu
    var = jnp.mean(zc * zc, axis=-1, keepdims=True)
    return zc * lax.rsqrt(var + LN_EPS) * g + b


def _dot(a, b):
    return jnp.dot(a, b, preferred_element_type=F32)


ROPE_PACK = LANES // HALF
ROPE_TILE = 2048


def _rope_table_kernel(pos_ref, inv_ref, sign_ref, cos_ref, sin_ref):
    rows = pos_ref.shape[0]
    ang = pos_ref[...] * inv_ref[...]
    cos = jnp.cos(ang)
    sin = jnp.sin(ang)
    first_group = lax.broadcasted_iota(jnp.int32, (rows, LANES), 1) < HALF

    def spread(t, j):
        x = t if j == 0 else pltpu.roll(t, LANES - HALF * j, 1)
        x = jnp.where(first_group, x, 0.0)
        x = x + pltpu.roll(x, HALF, 1)
        return x + pltpu.roll(x, 2 * HALF, 1)

    for j in range(ROPE_PACK):
        cos_ref[j * rows:(j + 1) * rows, :] = spread(cos, j)
        sin_ref[j * rows:(j + 1) * rows, :] = spread(sin, j) * sign_ref[...]


def _rope_tables(positions):
    n = positions.size
    tile = min(n, ROPE_TILE)
    rows = tile // ROPE_PACK
    inv = jnp.power(ROPE_THETA, -jnp.arange(0, HEAD_DIM, 2, dtype=F32) / HEAD_DIM)
    inv128 = jnp.tile(inv, ROPE_PACK).reshape(1, LANES)
    sign = jnp.where(jnp.arange(LANES) < LANES // 2, -1.0, 1.0).astype(F32).reshape(1, LANES)
    pos = positions.reshape(n // tile, ROPE_PACK, rows).astype(F32).transpose(0, 2, 1)
    pos = jnp.repeat(pos, HALF, axis=2).reshape(n // ROPE_PACK, LANES)
    packed = pl.BlockSpec((rows, LANES), lambda i: (i, 0))
    row = pl.BlockSpec((tile, LANES), lambda i: (i, 0))
    vec = pl.BlockSpec((1, LANES), lambda i: (0, 0))
    return pl.pallas_call(
        _rope_table_kernel,
        grid=(n // tile,),
        in_specs=[packed, vec, vec],
        out_specs=[row, row],
        out_shape=[jax.ShapeDtypeStruct((n, LANES), F32)] * 2,
        name="rope_tables",
    )(pos, inv128, sign)


FFN_CHUNK = 256
FFN_TOKEN_TILE = 1024
FFN_OUT_ROWS = 256


def _ffn_residual_ln(x, wi_ref, wo_ref, g_ref, b_ref, o_ref, act_ref, row0=0):
    xb = x.astype(BF16)
    n_rows = x.shape[0]
    for c in range(D_FF // FFN_CHUNK):
        lo = c * FFN_CHUNK
        gate = _dot(xb, wi_ref[:, lo:lo + FFN_CHUNK])
        up = _dot(xb, wi_ref[:, D_FF + lo:D_FF + lo + FFN_CHUNK])
        act = gate * (1.0 / (1.0 + jnp.exp(-gate))) * up
        act_ref[row0:row0 + n_rows, lo:lo + FFN_CHUNK] = act.astype(BF16)
    rows = min(FFN_OUT_ROWS, n_rows)
    for r in range(n_rows // rows):
        sl = slice(r * rows, (r + 1) * rows)
        out = slice(row0 + r * rows, row0 + (r + 1) * rows)
        y = _dot(act_ref[out, :], wo_ref[...])
        o_ref[out, :] = _layer_norm(ALPHA * x[sl, :] + 0.5 * y, g_ref[...], b_ref[...])


def _ffn_kernel(x_ref, wi_ref, wo_ref, g_ref, b_ref, o_ref, act_ref):
    _ffn_residual_ln(x_ref[...], wi_ref, wo_ref, g_ref, b_ref, o_ref, act_ref)


def _const_spec(shape):
    return pl.BlockSpec(shape, lambda i: (0, 0), pipeline_mode=pl.Buffered(1))


def _layer_spec(shape, layer):
    return pl.BlockSpec((None,) + shape, lambda i: (layer, 0, 0), pipeline_mode=pl.Buffered(1))


def _ffn_ln(x, w_in, w_out, g, b, layer, tm):
    n = x.shape[0]
    row = pl.BlockSpec((tm, D_MODEL), lambda i: (i, 0))
    return pl.pallas_call(
        _ffn_kernel,
        grid=(n // tm,),
        in_specs=[row, _layer_spec((D_MODEL, 2 * D_FF), layer), _layer_spec((D_FF, D_MODEL), layer),
                  _const_spec((1, D_MODEL)), _const_spec((1, D_MODEL))],
        out_specs=row,
        out_shape=jax.ShapeDtypeStruct((n, D_MODEL), F32),
        scratch_shapes=[pltpu.VMEM((tm, D_FF), BF16)],
        compiler_params=pltpu.CompilerParams(
            dimension_semantics=("parallel",), vmem_limit_bytes=VMEM_LIMIT),
        name="ffn_ln",
    )(x, w_in, w_out, g, b)


def _mix_ffn_kernel(x_ref, yd_ref, ys_ref, wm_ref, gm_ref, bm_ref,
                    wi_ref, wo_ref, g_ref, b_ref, o_ref, act_ref):
    half = x_ref.shape[0] // 2
    xs = []
    for h in range(2):
        sl = slice(h * half, (h + 1) * half)
        mix = _dot(jnp.concatenate([yd_ref[sl, :], ys_ref[sl, :]], axis=1), wm_ref[...])
        xs.append(_layer_norm(ALPHA * x_ref[sl, :] + mix, gm_ref[...], bm_ref[...]))
    for h in range(2):
        _ffn_residual_ln(xs[h], wi_ref, wo_ref, g_ref, b_ref, o_ref, act_ref, row0=h * half)


def _mix_ffn_ln(x, yd, ys, w_mix, g_mix, b_mix, w_in, w_out, g, b, layer, tm):
    n = x.shape[0]
    row = lambda width: pl.BlockSpec((tm, width), lambda i: (i, 0))
    vec = _const_spec((1, D_MODEL))
    return pl.pallas_call(
        _mix_ffn_kernel,
        grid=(n // tm,),
        in_specs=[row(D_MODEL), row(DV), row(SQ), _layer_spec((DV + SQ, D_MODEL), layer), vec, vec,
                  _layer_spec((D_MODEL, 2 * D_FF), layer), _layer_spec((D_FF, D_MODEL), layer),
                  vec, vec],
        out_specs=row(D_MODEL),
        out_shape=jax.ShapeDtypeStruct((n, D_MODEL), F32),
        scratch_shapes=[pltpu.VMEM((tm, D_FF), BF16)],
        compiler_params=pltpu.CompilerParams(
            dimension_semantics=("parallel",), vmem_limit_bytes=VMEM_LIMIT),
        name="mix_ffn_ln",
    )(x, yd, ys, w_mix, g_mix, b_mix, w_in, w_out, g, b)


def _proj_kernel(x_ref, w_ref, cos_ref, sin_ref, ones_ref,
                 qd_ref, kd_ref, vd_ref, qs_ref, ks_ref, vs_ref, norm_ref, *, tm):
    xb = x_ref[...].astype(BF16)
    cos = cos_ref[...]
    sin = sin_ref[...]

    def rope(t, scale):
        r = t * cos + pltpu.roll(t, LANES // 2, 1) * sin
        return r * scale if scale != 1.0 else r

    def put_transposed(ref, col, t):
        for blk in range(tm // BLOCK):
            ref[0, blk, col:col + LANES, :] = t[blk * BLOCK:(blk + 1) * BLOCK, :].T.astype(BF16)

    def put_rows(ref, col, t):
        ref[:, col:col + LANES] = t.astype(BF16)

    plan = []
    for j in range(DQ // LANES):
        plan.append((j * LANES, qd_ref, j * LANES, put_transposed, Q_SCALE))
    for j in range(DK // LANES):
        plan.append((DQ + j * LANES, kd_ref, j * LANES, put_rows, 1.0))
    for j in range(SQ // LANES):
        plan.append((DQ + DK + j * LANES, qs_ref, j * LANES, put_transposed, Q_SCALE))
    plan.append((DQ + DK + SQ, ks_ref, 0, put_rows, 1.0))
    for j in range(DV // LANES):
        plan.append((ROPE_COLS + j * LANES, vd_ref, j * LANES, put_transposed, None))
    plan.append((ROPE_COLS + DV, vs_ref, 0, put_transposed, None))

    squares = [None] * NORM_GROUPS
    norm_base = ((qd_ref, 0), (kd_ref, DIFF_HEADS), (qs_ref, 2 * DIFF_HEADS),
                 (ks_ref, 2 * DIFF_HEADS + SWA_GROUP))

    for first in range(0, len(plan), 2):
        lo = plan[first][0]
        t2 = _dot(xb, w_ref[:, lo:lo + 2 * LANES])
        for half, (_, ref, col, put, scale) in enumerate(plan[first:first + 2]):
            t = t2[:, half * LANES:(half + 1) * LANES]
            r = t if scale is None else rope(t, scale)
            put(ref, col, r)
            for normed_ref, base in norm_base:
                if ref is normed_ref:
                    squares[base + col // LANES] = (r * r).astype(BF16)

    norms = _dot(jnp.concatenate(squares, axis=1), ones_ref[...])
    norm_ref[...] = jnp.broadcast_to(jnp.max(norms, axis=0, keepdims=True), (SUBLANES, LANES))[None]


def _in_proj(x, w, cos, sin, layer, batch, seq, tm):
    n = x.shape[0]
    nq = seq // BLOCK
    tpb = seq // tm
    sub = tm // BLOCK
    row = lambda width: pl.BlockSpec((tm, width), lambda i: (i, 0))
    tblk = lambda rows: pl.BlockSpec((1, sub, rows, BLOCK), lambda i: (i // tpb, i % tpb, 0, 0))
    tshape = lambda rows: jax.ShapeDtypeStruct((batch, nq, rows, BLOCK), BF16)
    group_ones = (jnp.arange(NORM_GROUPS * LANES)[:, None] // LANES
                  == jnp.arange(LANES)[None, :]).astype(BF16)
    *proj, norms = pl.pallas_call(
        functools.partial(_proj_kernel, tm=tm),
        grid=(n // tm,),
        in_specs=[row(D_MODEL), _layer_spec((D_MODEL, IN_COLS), layer), row(LANES), row(LANES),
                  _const_spec((NORM_GROUPS * LANES, LANES))],
        out_specs=[tblk(DQ), row(DK), tblk(DV), tblk(SQ), row(SK), tblk(SV),
                   pl.BlockSpec((1, SUBLANES, LANES), lambda i: (i, 0, 0))],
        out_shape=[tshape(DQ), jax.ShapeDtypeStruct((n, DK), BF16), tshape(DV),
                   tshape(SQ), jax.ShapeDtypeStruct((n, SK), BF16), tshape(SV),
                   jax.ShapeDtypeStruct((n // tm, SUBLANES, LANES), F32)],
        compiler_params=pltpu.CompilerParams(
            dimension_semantics=("parallel",), vmem_limit_bytes=VMEM_LIMIT),
        name="in_proj_rope",
    )(x, w, cos, sin, group_ones)
    biggest = norms[:, 0, :NORM_GROUPS].reshape(batch, tpb, NORM_GROUPS).max(axis=1)
    q2, k2 = biggest[:, :DIFF_HEADS], biggest[:, DIFF_HEADS:2 * DIFF_HEADS]
    sq2, sk2 = biggest[:, 2 * DIFF_HEADS:NORM_GROUPS - 1].max(axis=1), biggest[:, NORM_GROUPS - 1]
    diff_bound = jnp.sqrt(q2 * k2) * BF16_ROUNDING_MARGIN
    swa_bound = jnp.sqrt(sq2 * sk2) * BF16_ROUNDING_MARGIN
    return proj, diff_bound, swa_bound


def _head_select_mask():
    r = lax.broadcasted_iota(jnp.int32, (LANES, 1), 0)
    return (r % HEAD_DIM) < HALF


def _pair_queries(qt, sel0):
    zero = jnp.zeros_like(qt)
    return jnp.concatenate([jnp.where(sel0, qt, zero), jnp.where(sel0, zero, qt)], axis=1)


DIFF_KV_TILE = 256
DIFF_SCORE_BUFFERS = 4
DIFF_BLOCKS_PER_STEP = 4
DIRECT_BLOCKS_PER_STEP = 16
DIRECT_LOOKAHEAD = 6


def _diff_attn_kernel(bound_ref, q_ref, k_ref, v_ref, lam_ref, prm_ref, y_ref, *s_refs, seq):
    lv = lam_ref[...]
    a1 = jnp.sum(lv[0:1, :] * lv[1:2, :], axis=-1, keepdims=True)
    a2 = jnp.sum(lv[2:3, :] * lv[3:4, :], axis=-1, keepdims=True)
    lam = jnp.exp(a1) - jnp.exp(a2) + prm_ref[2:3, 0:1]
    gain = prm_ref[0:1, :]
    one_minus_init = prm_ref[1:2, :]
    sel0 = _head_select_mask()
    tk = min(DIFF_KV_TILE, seq)
    nq = seq // BLOCK
    tiles = [(j * tk, (j + 1) * tk) for j in range(seq // tk)]

    def scores(i, s_ref):
        qm = _pair_queries(q_ref[0, i], sel0)
        m8 = None
        for lo, hi in tiles:
            s = _dot(k_ref[lo:hi, :], qm)
            s_ref[lo:hi, :] = s
            tile_max = jnp.max(s.reshape(tk // SUBLANES, SUBLANES, 2 * BLOCK), axis=0)
            m8 = tile_max if m8 is None else jnp.maximum(m8, tile_max)
        return jnp.max(m8, axis=0, keepdims=True)

    def weighted_values(p, lo, hi, o, l8):
        l8 = l8 + jnp.sum(p.reshape(tk // SUBLANES, SUBLANES, 2 * BLOCK), axis=0)
        vt = jnp.concatenate([v_ref[0, t] for t in range(lo // BLOCK, hi // BLOCK)], axis=1)
        return o + _dot(vt, p.astype(BF16)), l8

    def finish(i, s_ref, m):
        l8 = jnp.zeros((SUBLANES, 2 * BLOCK), F32)
        o = jnp.zeros((LANES, 2 * BLOCK), F32)
        for lo, hi in tiles:
            o, l8 = weighted_values(jnp.exp2(s_ref[lo:hi, :] - m), lo, hi, o, l8)
        write_block(i, o, l8)

    def write_block(i, o, l8):
        on = o * (1.0 / jnp.sum(l8, axis=0, keepdims=True))
        d = (on[:, :BLOCK] - lam * on[:, BLOCK:]).T
        ms = jnp.mean(d * d, axis=-1, keepdims=True)
        y = d * lax.rsqrt(ms + RMS_EPS) * gain * one_minus_init
        start = i * BLOCK if isinstance(i, int) else pl.multiple_of(i * BLOCK, BLOCK)
        y_ref[pl.ds(start, BLOCK), :] = y.astype(BF16)

    nbuf = len(s_refs)

    def shifted_path():
        group = DIFF_BLOCKS_PER_STEP if nq % DIFF_BLOCKS_PER_STEP == 0 else nbuf
        steps = nq // group

        def block_group(t, m, last=False):
            for k in range(group):
                i = group * t + k
                if not (last and k == group - 1):
                    m_next = scores(i + 1, s_refs[(k + 1) % nbuf])
                finish(i, s_refs[k % nbuf], m)
                m = m_next
            return m

        m = lax.fori_loop(0, steps - 1, block_group, scores(0, s_refs[0]))
        block_group(steps - 1, m, last=True)

    def direct_path():
        group = DIRECT_BLOCKS_PER_STEP if nq % DIRECT_BLOCKS_PER_STEP == 0 else nbuf
        steps = nq // group

        def direct_group(t, carry):
            blocks = [group * t + k for k in range(group)]
            qms = [None] * group
            acc = [None] * group
            pending = []

            def consume():
                k, lo, hi, s = pending.pop(0)
                o, l8 = acc[k] if acc[k] is not None else (
                    jnp.zeros((LANES, 2 * BLOCK), F32), jnp.zeros((SUBLANES, 2 * BLOCK), F32))
                acc[k] = weighted_values(jnp.exp2(s), lo, hi, o, l8)
                if hi == seq:
                    write_block(blocks[k], *acc[k])

            for k in range(group):
                qms[k] = _pair_queries(q_ref[0, blocks[k]], sel0)
                for lo, hi in tiles:
                    pending.append((k, lo, hi, _dot(k_ref[lo:hi, :], qms[k])))
                    if len(pending) > DIRECT_LOOKAHEAD:
                        consume()
            while pending:
                consume()
            return carry
        lax.fori_loop(0, steps, direct_group, 0)

    bound = bound_ref[pl.program_id(0), pl.program_id(1)]
    lax.cond(bound < DIRECT_SCORE_LIMIT, direct_path, shifted_path)


def _diff_attn(bound, qd, kd, vd, lam_vec, prm, batch, seq):
    nq = seq // BLOCK
    tspec = pl.BlockSpec((1, nq, LANES, BLOCK), lambda b, h: (b, 0, h, 0))
    kspec = pl.BlockSpec((seq, LANES), lambda b, h: (b, h))
    small = lambda shape: pl.BlockSpec(shape, lambda b, h: (0, 0))
    return pl.pallas_call(
        functools.partial(_diff_attn_kernel, seq=seq),
        grid=(batch, DIFF_HEADS),
        in_specs=[pl.BlockSpec(memory_space=pltpu.SMEM),
                  tspec, kspec, tspec, small((4, HEAD_DIM)), small((SUBLANES, LANES))],
        out_specs=pl.BlockSpec((seq, LANES), lambda b, h: (b, h)),
        out_shape=jax.ShapeDtypeStruct((batch * seq, DV), BF16),
        scratch_shapes=[pltpu.VMEM((seq, 2 * BLOCK), F32)] * DIFF_SCORE_BUFFERS,
        compiler_params=pltpu.CompilerParams(
            dimension_semantics=("parallel", "parallel"), vmem_limit_bytes=VMEM_LIMIT),
        name="diff_attn",
    )(bound, qd, kd, vd, lam_vec, prm)


SWA_BLOCKS_PER_STEP = 8
SWA_LOOKAHEAD = 6


def _swa_kernel(bound_ref, q_ref, k_ref, v_ref, sink_ref, y_ref, *, seq):
    nq = seq // BLOCK
    sel0 = _head_select_mask()
    r = lax.broadcasted_iota(jnp.int32, (BLOCK, BLOCK), 0)
    c = lax.broadcasted_iota(jnp.int32, (BLOCK, BLOCK), 1)
    prev_band = jnp.where(c <= r, 0.0, NEG).astype(F32)
    next_band = jnp.where(r <= c, 0.0, NEG).astype(F32)

    rows = lambda blk: pl.ds(pl.multiple_of(blk * BLOCK, BLOCK), BLOCK)
    unroll = SWA_BLOCKS_PER_STEP if nq % SWA_BLOCKS_PER_STEP == 0 else 1

    def load_block(n):
        n_prev = jnp.maximum(n - 1, 0)
        n_next = jnp.minimum(n + 1, nq - 1)
        edge_prev = jnp.where(n >= 1, 0.0, NEG).astype(F32)
        edge_next = jnp.where(n <= nq - 2, 0.0, NEG).astype(F32)
        bias_prev = jnp.concatenate([prev_band + edge_prev] * 2, axis=1)
        bias_next = jnp.concatenate([next_band + edge_next] * 2, axis=1)
        bias = (bias_prev, bias_next)
        kb = jnp.concatenate([k_ref[rows(n_prev), :], k_ref[rows(n), :], k_ref[rows(n_next), :]],
                             axis=0)
        vb = jnp.concatenate([v_ref[0, n_prev], v_ref[0, n], v_ref[0, n_next]], axis=1)
        qms = [_pair_queries(q_ref[0, n, g * LANES:(g + 1) * LANES, :], sel0)
               for g in range(SWA_GROUP)]
        return kb, vb, bias, qms

    def masked_scores(kb, qm, bias):
        s = _dot(kb, qm)
        return jnp.concatenate([s[:BLOCK] + bias[0], s[BLOCK:2 * BLOCK],
                                s[2 * BLOCK:] + bias[1]], axis=0)

    def step(t, carry, shifted):
        blocks = [unroll * t + u for u in range(unroll)]
        loaded = [load_block(n) for n in blocks]
        pending, outs = [], [[] for _ in blocks]

        def consume():
            u, vb, g, s = pending.pop(0)
            sink = sink_ref[:, g * 2 * BLOCK:(g + 1) * 2 * BLOCK]
            if shifted:
                m = jnp.maximum(jnp.max(s, axis=0, keepdims=True), sink)
                e = jnp.exp2(s - m)
                den = jnp.sum(e, axis=0, keepdims=True) + jnp.exp2(sink - m)
            else:
                e = jnp.exp2(s)
                den = jnp.sum(e, axis=0, keepdims=True) + jnp.exp2(sink)
            on = _dot(vb, e.astype(BF16)) * (1.0 / den)
            yt = jnp.concatenate([on[:HEAD_DIM, :BLOCK], on[HEAD_DIM:, BLOCK:]], axis=0)
            outs[u].append(yt.T.astype(BF16))
            if g == SWA_GROUP - 1:
                y_ref[rows(blocks[u]), :] = jnp.concatenate(outs[u], axis=1)

        for u, (kb, vb, bias, qms) in enumerate(loaded):
            for g, qm in enumerate(qms):
                pending.append((u, vb, g, masked_scores(kb, qm, bias)))
                if len(pending) > SWA_LOOKAHEAD:
                    consume()
        while pending:
            consume()
        return carry

    lax.cond(bound_ref[pl.program_id(0)] < DIRECT_SCORE_LIMIT,
             lambda: lax.fori_loop(0, nq // unroll, functools.partial(step, shifted=False), 0),
             lambda: lax.fori_loop(0, nq // unroll, functools.partial(step, shifted=True), 0))


def _swa_attn(bound, qs, ks, vs, sink_vec, batch, seq):
    nq = seq // BLOCK
    return pl.pallas_call(
        functools.partial(_swa_kernel, seq=seq),
        grid=(batch,),
        in_specs=[pl.BlockSpec(memory_space=pltpu.SMEM),
                  pl.BlockSpec((1, nq, SQ, BLOCK), lambda b: (b, 0, 0, 0)),
                  pl.BlockSpec((seq, SK), lambda b: (b, 0)),
                  pl.BlockSpec((1, nq, SV, BLOCK), lambda b: (b, 0, 0, 0)),
                  pl.BlockSpec((1, SWA_Q_HEADS * BLOCK), lambda b: (0, 0))],
        out_specs=pl.BlockSpec((seq, SQ), lambda b: (b, 0)),
        out_shape=jax.ShapeDtypeStruct((batch * seq, SQ), BF16),
        compiler_params=pltpu.CompilerParams(
            dimension_semantics=("parallel",), vmem_limit_bytes=VMEM_LIMIT),
        name="swa_attn",
    )(bound, qs, ks, vs, sink_vec)


def _token_tile(n, want):
    return want if n % want == 0 else BLOCK


def kernel(x, positions, w_in, w_out, diff_lambda, diff_subln_g, swa_sink,
           ffn1_w_in, ffn1_w_out, ffn2_w_in, ffn2_w_out, ln_g, ln_b):
    batch, seq, _ = x.shape
    n = batch * seq
    tm = _token_tile(seq, 1024)
    tm_ffn = _token_tile(n, FFN_TOKEN_TILE)
    h = x.reshape(n, D_MODEL)
    cos, sin = _rope_tables(positions)

    w_in_p = _permute_in_proj(w_in).astype(BF16)
    w_out_p = _permute_out_proj(w_out).astype(BF16)
    ffn1_in, ffn1_out = ffn1_w_in.astype(BF16), ffn1_w_out.astype(BF16)
    ffn2_in, ffn2_out = ffn2_w_in.astype(BF16), ffn2_w_out.astype(BF16)
    sink = swa_sink.reshape(DEPTH, SWA_KV_HEADS, SWA_GROUP).transpose(0, 2, 1) * LOG2E
    sink_vec = jnp.repeat(sink.reshape(DEPTH, 1, SWA_Q_HEADS), BLOCK, axis=2)

    for l in range(DEPTH):
        lambda_init = 0.8 - 0.6 * math.exp(-0.3 * l)
        g = lambda i: ln_g[l, i].reshape(1, D_MODEL)
        b = lambda i: ln_b[l, i].reshape(1, D_MODEL)
        prm = jnp.zeros((SUBLANES, LANES), F32)
        prm = prm.at[0].set(diff_subln_g[l]).at[1].set(1.0 - lambda_init).at[2].set(lambda_init)

        h = _ffn_ln(h, ffn1_in, ffn1_out, g(0), b(0), l, tm_ffn)
        (qd, kd, vd, qs, ks, vs), diff_bound, swa_bound = _in_proj(
            h, w_in_p, cos, sin, l, batch, seq, tm)
        swa_bound = jnp.maximum(swa_bound, jnp.max(jnp.abs(sink_vec[l])))
        yd = _diff_attn(diff_bound, qd, kd, vd, diff_lambda[l], prm, batch, seq)
        ys = _swa_attn(swa_bound, qs, ks, vs, sink_vec[l], batch, seq)
        h = _mix_ffn_ln(h, yd, ys, w_out_p, g(1), b(1), ffn2_in, ffn2_out, g(2), b(2), l, tm_ffn)
    return h.reshape(batch, seq, D_MODEL)
```

```python
import functools
import math

import jax
import jax.numpy as jnp
from jax import lax
from jax.experimental import pallas as pl
from jax.experimental.pallas import tpu as pltpu

D_MODEL = 1024
DEPTH = 4
HEAD_DIM = 64
HALF = HEAD_DIM // 2
DIFF_HEADS = 4
SWA_Q_HEADS = 8
SWA_KV_HEADS = 2
SWA_GROUP = SWA_Q_HEADS // SWA_KV_HEADS
BLOCK = 128
D_FF = 2816
ROPE_THETA = 10000.0
ALPHA = (2.0 * DEPTH) ** 0.25
LN_EPS = 1e-5
RMS_EPS = 1e-5
LOG2E = 1.4426950408889634
Q_SCALE = (HEAD_DIM ** -0.5) * LOG2E
NEG = -1e30

LANES = 128
SUBLANES = 8
DQ = DK = DV = 512
SQ = 512
SK = SV = 128
IN_COLS = DQ + DK + DV + SQ + SK + SV
ROPE_COLS = DQ + DK + SQ + SK

VMEM_LIMIT = 56 * 1024 * 1024

NORM_GROUPS = 2 * DIFF_HEADS + SWA_GROUP + 1
DIRECT_SCORE_LIMIT = 0.0
BF16_ROUNDING_MARGIN = 1.02

BF16 = jnp.bfloat16
F32 = jnp.float32


def _permute_in_proj(w):
    lead = w.shape[:-1]
    dq, dk, dv = w[..., :DQ], w[..., DQ:DQ + DK], w[..., DQ + DK:DQ + DK + DV]
    sq0 = DQ + DK + DV
    sq, sk, sv = w[..., sq0:sq0 + SQ], w[..., sq0 + SQ:sq0 + SQ + SK], w[..., sq0 + SQ + SK:]
    pair = lambda t: t.reshape(lead + (DIFF_HEADS, 2, 2, HALF)).swapaxes(-3, -2).reshape(lead + (DQ,))
    sq = jnp.moveaxis(sq.reshape(lead + (SWA_KV_HEADS, SWA_GROUP, 2, HALF)), -4, -2)
    sk = sk.reshape(lead + (SWA_KV_HEADS, 2, HALF)).swapaxes(-3, -2)
    return jnp.concatenate(
        [pair(dq), pair(dk), sq.reshape(lead + (SQ,)), sk.reshape(lead + (SK,)), dv, sv], axis=-1)


def _permute_out_proj(w):
    lead, tail = w.shape[:-2], w.shape[-1:]
    swa = w[..., DV:, :].reshape(lead + (SWA_KV_HEADS, SWA_GROUP, HEAD_DIM) + tail)
    swa = swa.swapaxes(-4, -3).reshape(lead + (SQ,) + tail)
    return jnp.concatenate([w[..., :DV, :], swa], axis=-2)


def _layer_norm(z, g, b):
    mu = jnp.mean(z, axis=-1, keepdims=True)
    zc = z - mu
    var = jnp.mean(zc * zc, axis=-1, keepdims=True)
    return zc * lax.rsqrt(var + LN_EPS) * g + b


def _dot(a, b):
    return jnp.dot(a, b, preferred_element_type=F32)


ROPE_PACK = LANES // HALF
ROPE_TILE = 2048


def _rope_table_kernel(pos_ref, inv_ref, sign_ref, cos_ref, sin_ref):
    rows = pos_ref.shape[0]
    ang = pos_ref[...] * inv_ref[...]
    cos = jnp.cos(ang)
    sin = jnp.sin(ang)
    first_group = lax.broadcasted_iota(jnp.int32, (rows, LANES), 1) < HALF

    def spread(t, j):
        x = t if j == 0 else pltpu.roll(t, LANES - HALF * j, 1)
        x = jnp.where(first_group, x, 0.0)
        x = x + pltpu.roll(x, HALF, 1)
        return x + pltpu.roll(x, 2 * HALF, 1)

    for j in range(ROPE_PACK):
        cos_ref[j * rows:(j + 1) * rows, :] = spread(cos, j)
        sin_ref[j * rows:(j + 1) * rows, :] = spread(sin, j) * sign_ref[...]


def _rope_tables(positions):
    n = positions.size
    tile = min(n, ROPE_TILE)
    rows = tile // ROPE_PACK
    inv = jnp.power(ROPE_THETA, -jnp.arange(0, HEAD_DIM, 2, dtype=F32) / HEAD_DIM)
    inv128 = jnp.tile(inv, ROPE_PACK).reshape(1, LANES)
    sign = jnp.where(jnp.arange(LANES) < LANES // 2, -1.0, 1.0).astype(F32).reshape(1, LANES)
    pos = positions.reshape(n // tile, ROPE_PACK, rows).astype(F32).transpose(0, 2, 1)
    pos = jnp.repeat(pos, HALF, axis=2).reshape(n // ROPE_PACK, LANES)
    packed = pl.BlockSpec((rows, LANES), lambda i: (i, 0))
    row = pl.BlockSpec((tile, LANES), lambda i: (i, 0))
    vec = pl.BlockSpec((1, LANES), lambda i: (0, 0))
    return pl.pallas_call(
        _rope_table_kernel,
        grid=(n // tile,),
        in_specs=[packed, vec, vec],
        out_specs=[row, row],
        out_shape=[jax.ShapeDtypeStruct((n, LANES), F32)] * 2,
        name="rope_tables",
    )(pos, inv128, sign)


FFN_CHUNK = 256
FFN_TOKEN_TILE = 1024
FFN_OUT_ROWS = 256


def _ffn_residual_ln(x, wi_ref, wo_ref, g_ref, b_ref, o_ref, act_ref, row0=0):
    xb = x.astype(BF16)
    n_rows = x.shape[0]
    for c in range(D_FF // FFN_CHUNK):
        lo = c * FFN_CHUNK
        gate = _dot(xb, wi_ref[:, lo:lo + FFN_CHUNK])
        up = _dot(xb, wi_ref[:, D_FF + lo:D_FF + lo + FFN_CHUNK])
        act = gate * (1.0 / (1.0 + jnp.exp(-gate))) * up
        act_ref[row0:row0 + n_rows, lo:lo + FFN_CHUNK] = act.astype(BF16)
    rows = min(FFN_OUT_ROWS, n_rows)
    for r in range(n_rows // rows):
        sl = slice(r * rows, (r + 1) * rows)
        out = slice(row0 + r * rows, row0 + (r + 1) * rows)
        y = _dot(act_ref[out, :], wo_ref[...])
        o_ref[out, :] = _layer_norm(ALPHA * x[sl, :] + 0.5 * y, g_ref[...], b_ref[...])


def _ffn_kernel(x_ref, wi_ref, wo_ref, g_ref, b_ref, o_ref, act_ref):
    _ffn_residual_ln(x_ref[...], wi_ref, wo_ref, g_ref, b_ref, o_ref, act_ref)


def _const_spec(shape):
    return pl.BlockSpec(shape, lambda i: (0, 0), pipeline_mode=pl.Buffered(1))


def _layer_spec(shape, layer):
    return pl.BlockSpec((None,) + shape, lambda i: (layer, 0, 0), pipeline_mode=pl.Buffered(1))


def _ffn_ln(x, w_in, w_out, g, b, layer, tm):
    n = x.shape[0]
    row = pl.BlockSpec((tm, D_MODEL), lambda i: (i, 0))
    return pl.pallas_call(
        _ffn_kernel,
        grid=(n // tm,),
        in_specs=[row, _layer_spec((D_MODEL, 2 * D_FF), layer), _layer_spec((D_FF, D_MODEL), layer),
                  _const_spec((1, D_MODEL)), _const_spec((1, D_MODEL))],
        out_specs=row,
        out_shape=jax.ShapeDtypeStruct((n, D_MODEL), F32),
        scratch_shapes=[pltpu.VMEM((tm, D_FF), BF16)],
        compiler_params=pltpu.CompilerParams(
            dimension_semantics=("parallel",), vmem_limit_bytes=VMEM_LIMIT),
        name="ffn_ln",
    )(x, w_in, w_out, g, b)


def _mix_ffn_kernel(x_ref, yd_ref, ys_ref, wm_ref, gm_ref, bm_ref,
                    wi_ref, wo_ref, g_ref, b_ref, o_ref, act_ref):
    half = x_ref.shape[0] // 2
    xs = []
    for h in range(2):
        sl = slice(h * half, (h + 1) * half)
        mix = _dot(jnp.concatenate([yd_ref[sl, :], ys_ref[sl, :]], axis=1), wm_ref[...])
        xs.append(_layer_norm(ALPHA * x_ref[sl, :] + mix, gm_ref[...], bm_ref[...]))
    for h in range(2):
        _ffn_residual_ln(xs[h], wi_ref, wo_ref, g_ref, b_ref, o_ref, act_ref, row0=h * half)


def _mix_ffn_ln(x, yd, ys, w_mix, g_mix, b_mix, w_in, w_out, g, b, layer, tm):
    n = x.shape[0]
    row = lambda width: pl.BlockSpec((tm, width), lambda i: (i, 0))
    vec = _const_spec((1, D_MODEL))
    return pl.pallas_call(
        _mix_ffn_kernel,
        grid=(n // tm,),
        in_specs=[row(D_MODEL), row(DV), row(SQ), _layer_spec((DV + SQ, D_MODEL), layer), vec, vec,
                  _layer_spec((D_MODEL, 2 * D_FF), layer), _layer_spec((D_FF, D_MODEL), layer),
                  vec, vec],
        out_specs=row(D_MODEL),
        out_shape=jax.ShapeDtypeStruct((n, D_MODEL), F32),
        scratch_shapes=[pltpu.VMEM((tm, D_FF), BF16)],
        compiler_params=pltpu.CompilerParams(
            dimension_semantics=("parallel",), vmem_limit_bytes=VMEM_LIMIT),
        name="mix_ffn_ln",
    )(x, yd, ys, w_mix, g_mix, b_mix, w_in, w_out, g, b)


def _proj_kernel(x_ref, w_ref, cos_ref, sin_ref, ones_ref,
                 qd_ref, kd_ref, vd_ref, qs_ref, ks_ref, vs_ref, norm_ref, *, tm):
    xb = x_ref[...].astype(BF16)
    cos = cos_ref[...]
    sin = sin_ref[...]

    def rope(t, scale):
        r = t * cos + pltpu.roll(t, LANES // 2, 1) * sin
        return r * scale if scale != 1.0 else r

    def put_transposed(ref, col, t):
        for blk in range(tm // BLOCK):
            ref[0, blk, col:col + LANES, :] = t[blk * BLOCK:(blk + 1) * BLOCK, :].T.astype(BF16)

    def put_rows(ref, col, t):
        ref[:, col:col + LANES] = t.astype(BF16)

    plan = []
    for j in range(DQ // LANES):
        plan.append((j * LANES, qd_ref, j * LANES, put_transposed, Q_SCALE))
    for j in range(DK // LANES):
        plan.append((DQ + j * LANES, kd_ref, j * LANES, put_rows, 1.0))
    for j in range(SQ // LANES):
        plan.append((DQ + DK + j * LANES, qs_ref, j * LANES, put_transposed, Q_SCALE))
    plan.append((DQ + DK + SQ, ks_ref, 0, put_rows, 1.0))
    for j in range(DV // LANES):
        plan.append((ROPE_COLS + j * LANES, vd_ref, j * LANES, put_transposed, None))
    plan.append((ROPE_COLS + DV, vs_ref, 0, put_transposed, None))

    squares = [None] * NORM_GROUPS
    norm_base = ((qd_ref, 0), (kd_ref, DIFF_HEADS), (qs_ref, 2 * DIFF_HEADS),
                 (ks_ref, 2 * DIFF_HEADS + SWA_GROUP))

    for first in range(0, len(plan), 2):
        lo = plan[first][0]
        t2 = _dot(xb, w_ref[:, lo:lo + 2 * LANES])
        for half, (_, ref, col, put, scale) in enumerate(plan[first:first + 2]):
            t = t2[:, half * LANES:(half + 1) * LANES]
            r = t if scale is None else rope(t, scale)
            put(ref, col, r)
            for normed_ref, base in norm_base:
                if ref is normed_ref:
                    squares[base + col // LANES] = (r * r).astype(BF16)

    norms = _dot(jnp.concatenate(squares, axis=1), ones_ref[...])
    norm_ref[...] = jnp.broadcast_to(jnp.max(norms, axis=0, keepdims=True), (SUBLANES, LANES))[None]


def _in_proj(x, w, cos, sin, layer, batch, seq, tm):
    n = x.shape[0]
    nq = seq // BLOCK
    tpb = seq // tm
    sub = tm // BLOCK
    row = lambda width: pl.BlockSpec((tm, width), lambda i: (i, 0))
    tblk = lambda rows: pl.BlockSpec((1, sub, rows, BLOCK), lambda i: (i // tpb, i % tpb, 0, 0))
    tshape = lambda rows: jax.ShapeDtypeStruct((batch, nq, rows, BLOCK), BF16)
    group_ones = (jnp.arange(NORM_GROUPS * LANES)[:, None] // LANES
                  == jnp.arange(LANES)[None, :]).astype(BF16)
    *proj, norms = pl.pallas_call(
        functools.partial(_proj_kernel, tm=tm),
        grid=(n // tm,),
        in_specs=[row(D_MODEL), _layer_spec((D_MODEL, IN_COLS), layer), row(LANES), row(LANES),
                  _const_spec((NORM_GROUPS * LANES, LANES))],
        out_specs=[tblk(DQ), row(DK), tblk(DV), tblk(SQ), row(SK), tblk(SV),
                   pl.BlockSpec((1, SUBLANES, LANES), lambda i: (i, 0, 0))],
        out_shape=[tshape(DQ), jax.ShapeDtypeStruct((n, DK), BF16), tshape(DV),
                   tshape(SQ), jax.ShapeDtypeStruct((n, SK), BF16), tshape(SV),
                   jax.ShapeDtypeStruct((n // tm, SUBLANES, LANES), F32)],
        compiler_params=pltpu.CompilerParams(
            dimension_semantics=("parallel",), vmem_limit_bytes=VMEM_LIMIT),
        name="in_proj_rope",
    )(x, w, cos, sin, group_ones)
    biggest = norms[:, 0, :NORM_GROUPS].reshape(batch, tpb, NORM_GROUPS).max(axis=1)
    q2, k2 = biggest[:, :DIFF_HEADS], biggest[:, DIFF_HEADS:2 * DIFF_HEADS]
    sq2, sk2 = biggest[:, 2 * DIFF_HEADS:NORM_GROUPS - 1].max(axis=1), biggest[:, NORM_GROUPS - 1]
    diff_bound = jnp.sqrt(q2 * k2) * BF16_ROUNDING_MARGIN
    swa_bound = jnp.sqrt(sq2 * sk2) * BF16_ROUNDING_MARGIN
    return proj, diff_bound, swa_bound


def _head_select_mask():
    r = lax.broadcasted_iota(jnp.int32, (LANES, 1), 0)
    return (r % HEAD_DIM) < HALF


def _pair_queries(qt, sel0):
    zero = jnp.zeros_like(qt)
    return jnp.concatenate([jnp.where(sel0, qt, zero), jnp.where(sel0, zero, qt)], axis=1)


DIFF_KV_TILE = 256
DIFF_SCORE_BUFFERS = 4
DIFF_BLOCKS_PER_STEP = 4
DIRECT_BLOCKS_PER_STEP = 16
DIRECT_LOOKAHEAD = 6


def _diff_attn_kernel(bound_ref, q_ref, k_ref, v_ref, lam_ref, prm_ref, y_ref, *s_refs, seq):
    lv = lam_ref[...]
    a1 = jnp.sum(lv[0:1, :] * lv[1:2, :], axis=-1, keepdims=True)
    a2 = jnp.sum(lv[2:3, :] * lv[3:4, :], axis=-1, keepdims=True)
    lam = jnp.exp(a1) - jnp.exp(a2) + prm_ref[2:3, 0:1]
    gain = prm_ref[0:1, :]
    one_minus_init = prm_ref[1:2, :]
    sel0 = _head_select_mask()
    tk = min(DIFF_KV_TILE, seq)
    nq = seq // BLOCK
    tiles = [(j * tk, (j + 1) * tk) for j in range(seq // tk)]

    def scores(i, s_ref):
        qm = _pair_queries(q_ref[0, i], sel0)
        m8 = None
        for lo, hi in tiles:
            s = _dot(k_ref[lo:hi, :], qm)
            s_ref[lo:hi, :] = s
            tile_max = jnp.max(s.reshape(tk // SUBLANES, SUBLANES, 2 * BLOCK), axis=0)
            m8 = tile_max if m8 is None else jnp.maximum(m8, tile_max)
        return jnp.max(m8, axis=0, keepdims=True)

    def weighted_values(p, lo, hi, o, l8):
        l8 = l8 + jnp.sum(p.reshape(tk // SUBLANES, SUBLANES, 2 * BLOCK), axis=0)
        vt = jnp.concatenate([v_ref[0, t] for t in range(lo // BLOCK, hi // BLOCK)], axis=1)
        return o + _dot(vt, p.astype(BF16)), l8

    def finish(i, s_ref, m):
        l8 = jnp.zeros((SUBLANES, 2 * BLOCK), F32)
        o = jnp.zeros((LANES, 2 * BLOCK), F32)
        for lo, hi in tiles:
            o, l8 = weighted_values(jnp.exp2(s_ref[lo:hi, :] - m), lo, hi, o, l8)
        write_block(i, o, l8)

    def write_block(i, o, l8):
        on = o * (1.0 / jnp.sum(l8, axis=0, keepdims=True))
        d = (on[:, :BLOCK] - lam * on[:, BLOCK:]).T
        ms = jnp.mean(d * d, axis=-1, keepdims=True)
        y = d * lax.rsqrt(ms + RMS_EPS) * gain * one_minus_init
        start = i * BLOCK if isinstance(i, int) else pl.multiple_of(i * BLOCK, BLOCK)
        y_ref[pl.ds(start, BLOCK), :] = y.astype(BF16)

    nbuf = len(s_refs)

    def shifted_path():
        group = DIFF_BLOCKS_PER_STEP if nq % DIFF_BLOCKS_PER_STEP == 0 else nbuf
        steps = nq // group

        def block_group(t, m, last=False):
            for k in range(group):
                i = group * t + k
                if not (last and k == group - 1):
                    m_next = scores(i + 1, s_refs[(k + 1) % nbuf])
                finish(i, s_refs[k % nbuf], m)
                m = m_next
            return m

        m = lax.fori_loop(0, steps - 1, block_group, scores(0, s_refs[0]))
        block_group(steps - 1, m, last=True)

    def direct_path():
        group = DIRECT_BLOCKS_PER_STEP if nq % DIRECT_BLOCKS_PER_STEP == 0 else nbuf
        steps = nq // group

        def direct_group(t, carry):
            blocks = [group * t + k for k in range(group)]
            qms = [None] * group
            acc = [None] * group
            pending = []

            def consume():
                k, lo, hi, s = pending.pop(0)
                o, l8 = acc[k] if acc[k] is not None else (
                    jnp.zeros((LANES, 2 * BLOCK), F32), jnp.zeros((SUBLANES, 2 * BLOCK), F32))
                acc[k] = weighted_values(jnp.exp2(s), lo, hi, o, l8)
                if hi == seq:
                    write_block(blocks[k], *acc[k])

            for k in range(group):
                qms[k] = _pair_queries(q_ref[0, blocks[k]], sel0)
                for lo, hi in tiles:
                    pending.append((k, lo, hi, _dot(k_ref[lo:hi, :], qms[k])))
                    if len(pending) > DIRECT_LOOKAHEAD:
                        consume()
            while pending:
                consume()
            return carry
        lax.fori_loop(0, steps, direct_group, 0)

    bound = bound_ref[pl.program_id(0), pl.program_id(1)]
    lax.cond(bound < DIRECT_SCORE_LIMIT, direct_path, shifted_path)


def _diff_attn(bound, qd, kd, vd, lam_vec, prm, batch, seq):
    nq = seq // BLOCK
    tspec = pl.BlockSpec((1, nq, LANES, BLOCK), lambda b, h: (b, 0, h, 0))
    kspec = pl.BlockSpec((seq, LANES), lambda b, h: (b, h))
    small = lambda shape: pl.BlockSpec(shape, lambda b, h: (0, 0))
    return pl.pallas_call(
        functools.partial(_diff_attn_kernel, seq=seq),
        grid=(batch, DIFF_HEADS),
        in_specs=[pl.BlockSpec(memory_space=pltpu.SMEM),
                  tspec, kspec, tspec, small((4, HEAD_DIM)), small((SUBLANES, LANES))],
        out_specs=pl.BlockSpec((seq, LANES), lambda b, h: (b, h)),
        out_shape=jax.ShapeDtypeStruct((batch * seq, DV), BF16),
        scratch_shapes=[pltpu.VMEM((seq, 2 * BLOCK), F32)] * DIFF_SCORE_BUFFERS,
        compiler_params=pltpu.CompilerParams(
            dimension_semantics=("parallel", "parallel"), vmem_limit_bytes=VMEM_LIMIT),
        name="diff_attn",
    )(bound, qd, kd, vd, lam_vec, prm)


SWA_BLOCKS_PER_STEP = 8
SWA_LOOKAHEAD = 6


def _swa_kernel(bound_ref, q_ref, k_ref, v_ref, sink_ref, y_ref, *, seq):
    nq = seq // BLOCK
    sel0 = _head_select_mask()
    r = lax.broadcasted_iota(jnp.int32, (BLOCK, BLOCK), 0)
    c = lax.broadcasted_iota(jnp.int32, (BLOCK, BLOCK), 1)
    prev_band = jnp.where(c <= r, 0.0, NEG).astype(F32)
    next_band = jnp.where(r <= c, 0.0, NEG).astype(F32)

    rows = lambda blk: pl.ds(pl.multiple_of(blk * BLOCK, BLOCK), BLOCK)
    unroll = SWA_BLOCKS_PER_STEP if nq % SWA_BLOCKS_PER_STEP == 0 else 1

    def load_block(n):
        n_prev = jnp.maximum(n - 1, 0)
        n_next = jnp.minimum(n + 1, nq - 1)
        edge_prev = jnp.where(n >= 1, 0.0, NEG).astype(F32)
        edge_next = jnp.where(n <= nq - 2, 0.0, NEG).astype(F32)
        bias_prev = jnp.concatenate([prev_band + edge_prev] * 2, axis=1)
        bias_next = jnp.concatenate([next_band + edge_next] * 2, axis=1)
        bias = (bias_prev, bias_next)
        kb = jnp.concatenate([k_ref[rows(n_prev), :], k_ref[rows(n), :], k_ref[rows(n_next), :]],
                             axis=0)
        vb = jnp.concatenate([v_ref[0, n_prev], v_ref[0, n], v_ref[0, n_next]], axis=1)
        qms = [_pair_queries(q_ref[0, n, g * LANES:(g + 1) * LANES, :], sel0)
               for g in range(SWA_GROUP)]
        return kb, vb, bias, qms

    def masked_scores(kb, qm, bias):
        s = _dot(kb, qm)
        return jnp.concatenate([s[:BLOCK] + bias[0], s[BLOCK:2 * BLOCK],
                                s[2 * BLOCK:] + bias[1]], axis=0)

    def step(t, carry, shifted):
        blocks = [unroll * t + u for u in range(unroll)]
        loaded = [load_block(n) for n in blocks]
        pending, outs = [], [[] for _ in blocks]

        def consume():
            u, vb, g, s = pending.pop(0)
            sink = sink_ref[:, g * 2 * BLOCK:(g + 1) * 2 * BLOCK]
            if shifted:
                m = jnp.maximum(jnp.max(s, axis=0, keepdims=True), sink)
                e = jnp.exp2(s - m)
                den = jnp.sum(e, axis=0, keepdims=True) + jnp.exp2(sink - m)
            else:
                e = jnp.exp2(s)
                den = jnp.sum(e, axis=0, keepdims=True) + jnp.exp2(sink)
            on = _dot(vb, e.astype(BF16)) * (1.0 / den)
            yt = jnp.concatenate([on[:HEAD_DIM, :BLOCK], on[HEAD_DIM:, BLOCK:]], axis=0)
            outs[u].append(yt.T.astype(BF16))
            if g == SWA_GROUP - 1:
                y_ref[rows(blocks[u]), :] = jnp.concatenate(outs[u], axis=1)

        for u, (kb, vb, bias, qms) in enumerate(loaded):
            for g, qm in enumerate(qms):
                pending.append((u, vb, g, masked_scores(kb, qm, bias)))
                if len(pending) > SWA_LOOKAHEAD:
                    consume()
        while pending:
            consume()
        return carry

    lax.cond(bound_ref[pl.program_id(0)] < DIRECT_SCORE_LIMIT,
             lambda: lax.fori_loop(0, nq // unroll, functools.partial(step, shifted=False), 0),
             lambda: lax.fori_loop(0, nq // unroll, functools.partial(step, shifted=True), 0))


def _swa_attn(bound, qs, ks, vs, sink_vec, batch, seq):
    nq = seq // BLOCK
    return pl.pallas_call(
        functools.partial(_swa_kernel, seq=seq),
        grid=(batch,),
        in_specs=[pl.BlockSpec(memory_space=pltpu.SMEM),
                  pl.BlockSpec((1, nq, SQ, BLOCK), lambda b: (b, 0, 0, 0)),
                  pl.BlockSpec((seq, SK), lambda b: (b, 0)),
                  pl.BlockSpec((1, nq, SV, BLOCK), lambda b: (b, 0, 0, 0)),
                  pl.BlockSpec((1, SWA_Q_HEADS * BLOCK), lambda b: (0, 0))],
        out_specs=pl.BlockSpec((seq, SQ), lambda b: (b, 0)),
        out_shape=jax.ShapeDtypeStruct((batch * seq, SQ), BF16),
        compiler_params=pltpu.CompilerParams(
            dimension_semantics=("parallel",), vmem_limit_bytes=VMEM_LIMIT),
        name="swa_attn",
    )(bound, qs, ks, vs, sink_vec)


def _token_tile(n, want):
    return want if n % want == 0 else BLOCK


def kernel(x, positions, w_in, w_out, diff_lambda, diff_subln_g, swa_sink,
           ffn1_w_in, ffn1_w_out, ffn2_w_in, ffn2_w_out, ln_g, ln_b):
    batch, seq, _ = x.shape
    n = batch * seq
    tm = _token_tile(seq, 1024)
    tm_ffn = _token_tile(n, FFN_TOKEN_TILE)
    h = x.reshape(n, D_MODEL)
    cos, sin = _rope_tables(positions)

    w_in_p = _permute_in_proj(w_in).astype(BF16)
    w_out_p = _permute_out_proj(w_out).astype(BF16)
    ffn1_in, ffn1_out = ffn1_w_in.astype(BF16), ffn1_w_out.astype(BF16)
    ffn2_in, ffn2_out = ffn2_w_in.astype(BF16), ffn2_w_out.astype(BF16)
    sink = swa_sink.reshape(DEPTH, SWA_KV_HEADS, SWA_GROUP).transpose(0, 2, 1) * LOG2E
    sink_vec = jnp.repeat(sink.reshape(DEPTH, 1, SWA_Q_HEADS), BLOCK, axis=2)

    for l in range(DEPTH):
        lambda_init = 0.8 - 0.6 * math.exp(-0.3 * l)
        g = lambda i: ln_g[l, i].reshape(1, D_MODEL)
        b = lambda i: ln_b[l, i].reshape(1, D_MODEL)
        prm = jnp.zeros((SUBLANES, LANES), F32)
        prm = prm.at[0].set(diff_subln_g[l]).at[1].set(1.0 - lambda_init).at[2].set(lambda_init)

        h = _ffn_ln(h, ffn1_in, ffn1_out, g(0), b(0), l, tm_ffn)
        (qd, kd, vd, qs, ks, vs), diff_bound, swa_bound = _in_proj(
            h, w_in_p, cos, sin, l, batch, seq, tm)
        swa_bound = jnp.maximum(swa_bound, jnp.max(jnp.abs(sink_vec[l])))
        yd = _diff_attn(diff_bound, qd, kd, vd, diff_lambda[l], prm, batch, seq)
        ys = _swa_attn(swa_bound, qs, ks, vs, sink_vec[l], batch, seq)
        h = _mix_ffn_ln(h, yd, ys, w_out_p, g(1), b(1), ffn2_in, ffn2_out, g(2), b(2), l, tm_ffn)
    return h.reshape(batch, seq, D_MODEL)
```

```python
import functools
import math

import jax
import jax.numpy as jnp
from jax import lax
from jax.experimental import pallas as pl
from jax.experimental.pallas import tpu as pltpu

D_MODEL = 1024
DEPTH = 4
HEAD_DIM = 64
HALF = HEAD_DIM // 2
DIFF_HEADS = 4
SWA_Q_HEADS = 8
SWA_KV_HEADS = 2
SWA_GROUP = SWA_Q_HEADS // SWA_KV_HEADS
BLOCK = 128
D_FF = 2816
ROPE_THETA = 10000.0
ALPHA = (2.0 * DEPTH) ** 0.25
LN_EPS = 1e-5
RMS_EPS = 1e-5
LOG2E = 1.4426950408889634
Q_SCALE = (HEAD_DIM ** -0.5) * LOG2E
NEG = -1e30

LANES = 128
SUBLANES = 8
DQ = DK = DV = 512
SQ = 512
SK = SV = 128
IN_COLS = DQ + DK + DV + SQ + SK + SV
ROPE_COLS = DQ + DK + SQ + SK

VMEM_LIMIT = 56 * 1024 * 1024

NORM_GROUPS = 2 * DIFF_HEADS + SWA_GROUP + 1
DIRECT_SCORE_LIMIT = 64.0
BF16_ROUNDING_MARGIN = 1.02

BF16 = jnp.bfloat16
F32 = jnp.float32


def _permute_in_proj(w):
    lead = w.shape[:-1]
    dq, dk, dv = w[..., :DQ], w[..., DQ:DQ + DK], w[..., DQ + DK:DQ + DK + DV]
    sq0 = DQ + DK + DV
    sq, sk, sv = w[..., sq0:sq0 + SQ], w[..., sq0 + SQ:sq0 + SQ + SK], w[..., sq0 + SQ + SK:]
    pair = lambda t: t.reshape(lead + (DIFF_HEADS, 2, 2, HALF)).swapaxes(-3, -2).reshape(lead + (DQ,))
    sq = jnp.moveaxis(sq.reshape(lead + (SWA_KV_HEADS, SWA_GROUP, 2, HALF)), -4, -2)
    sk = sk.reshape(lead + (SWA_KV_HEADS, 2, HALF)).swapaxes(-3, -2)
    return jnp.concatenate(
        [pair(dq), pair(dk), sq.reshape(lead + (SQ,)), sk.reshape(lead + (SK,)), dv, sv], axis=-1)


def _permute_out_proj(w):
    lead, tail = w.shape[:-2], w.shape[-1:]
    swa = w[..., DV:, :].reshape(lead + (SWA_KV_HEADS, SWA_GROUP, HEAD_DIM) + tail)
    swa = swa.swapaxes(-4, -3).reshape(lead + (SQ,) + tail)
    return jnp.concatenate([w[..., :DV, :], swa], axis=-2)


def _layer_norm(z, g, b):
    mu = jnp.mean(z, axis=-1, keepdims=True)
    zc = z - mu
    var = jnp.mean(zc * zc, axis=-1, keepdims=True)
    return zc * lax.rsqrt(var + LN_EPS) * g + b


def _dot(a, b):
    return jnp.dot(a, b, preferred_element_type=F32)


ROPE_PACK = LANES // HALF
ROPE_TILE = 2048


def _rope_table_kernel(pos_ref, inv_ref, sign_ref, cos_ref, sin_ref):
    rows = pos_ref.shape[0]
    ang = pos_ref[...] * inv_ref[...]
    cos = jnp.cos(ang)
    sin = jnp.sin(ang)
    first_group = lax.broadcasted_iota(jnp.int32, (rows, LANES), 1) < HALF

    def spread(t, j):
        x = t if j == 0 else pltpu.roll(t, LANES - HALF * j, 1)
        x = jnp.where(first_group, x, 0.0)
        x = x + pltpu.roll(x, HALF, 1)
        return x + pltpu.roll(x, 2 * HALF, 1)

    for j in range(ROPE_PACK):
        cos_ref[j * rows:(j + 1) * rows, :] = spread(cos, j)
        sin_ref[j * rows:(j + 1) * rows, :] = spread(sin, j) * sign_ref[...]


def _rope_tables(positions):
    n = positions.size
    tile = min(n, ROPE_TILE)
    rows = tile // ROPE_PACK
    inv = jnp.power(ROPE_THETA, -jnp.arange(0, HEAD_DIM, 2, dtype=F32) / HEAD_DIM)
    inv128 = jnp.tile(inv, ROPE_PACK).reshape(1, LANES)
    sign = jnp.where(jnp.arange(LANES) < LANES // 2, -1.0, 1.0).astype(F32).reshape(1, LANES)
    pos = positions.reshape(n // tile, ROPE_PACK, rows).astype(F32).transpose(0, 2, 1)
    pos = jnp.repeat(pos, HALF, axis=2).reshape(n // ROPE_PACK, LANES)
    packed = pl.BlockSpec((rows, LANES), lambda i: (i, 0))
    row = pl.BlockSpec((tile, LANES), lambda i: (i, 0))
    vec = pl.BlockSpec((1, LANES), lambda i: (0, 0))
    return pl.pallas_call(
        _rope_table_kernel,
        grid=(n // tile,),
        in_specs=[packed, vec, vec],
        out_specs=[row, row],
        out_shape=[jax.ShapeDtypeStruct((n, LANES), F32)] * 2,
        name="rope_tables",
    )(pos, inv128, sign)


FFN_CHUNK = 256
FFN_TOKEN_TILE = 1024
FFN_OUT_ROWS = 256


def _ffn_residual_ln(x, wi_ref, wo_ref, g_ref, b_ref, o_ref, act_ref, row0=0):
    xb = x.astype(BF16)
    n_rows = x.shape[0]
    for c in range(D_FF // FFN_CHUNK):
        lo = c * FFN_CHUNK
        gate = _dot(xb, wi_ref[:, lo:lo + FFN_CHUNK])
        up = _dot(xb, wi_ref[:, D_FF + lo:D_FF + lo + FFN_CHUNK])
        act = gate * (1.0 / (1.0 + jnp.exp(-gate))) * up
        act_ref[row0:row0 + n_rows, lo:lo + FFN_CHUNK] = act.astype(BF16)
    rows = min(FFN_OUT_ROWS, n_rows)
    for r in range(n_rows // rows):
        sl = slice(r * rows, (r + 1) * rows)
        out = slice(row0 + r * rows, row0 + (r + 1) * rows)
        y = _dot(act_ref[out, :], wo_ref[...])
        o_ref[out, :] = _layer_norm(ALPHA * x[sl, :] + 0.5 * y, g_ref[...], b_ref[...])


def _ffn_kernel(x_ref, wi_ref, wo_ref, g_ref, b_ref, o_ref, act_ref):
    _ffn_residual_ln(x_ref[...], wi_ref, wo_ref, g_ref, b_ref, o_ref, act_ref)


def _const_spec(shape):
    return pl.BlockSpec(shape, lambda i: (0, 0), pipeline_mode=pl.Buffered(1))


def _layer_spec(shape, layer):
    return pl.BlockSpec((None,) + shape, lambda i: (layer, 0, 0), pipeline_mode=pl.Buffered(1))


def _ffn_ln(x, w_in, w_out, g, b, layer, tm):
    n = x.shape[0]
    row = pl.BlockSpec((tm, D_MODEL), lambda i: (i, 0))
    return pl.pallas_call(
        _ffn_kernel,
        grid=(n // tm,),
        in_specs=[row, _layer_spec((D_MODEL, 2 * D_FF), layer), _layer_spec((D_FF, D_MODEL), layer),
                  _const_spec((1, D_MODEL)), _const_spec((1, D_MODEL))],
        out_specs=row,
        out_shape=jax.ShapeDtypeStruct((n, D_MODEL), F32),
        scratch_shapes=[pltpu.VMEM((tm, D_FF), BF16)],
        compiler_params=pltpu.CompilerParams(
            dimension_semantics=("parallel",), vmem_limit_bytes=VMEM_LIMIT),
        name="ffn_ln",
    )(x, w_in, w_out, g, b)


def _mix_ffn_kernel(x_ref, yd_ref, ys_ref, wm_ref, gm_ref, bm_ref,
                    wi_ref, wo_ref, g_ref, b_ref, o_ref, act_ref):
    half = x_ref.shape[0] // 2
    xs = []
    for h in range(2):
        sl = slice(h * half, (h + 1) * half)
        mix = _dot(jnp.concatenate([yd_ref[sl, :], ys_ref[sl, :]], axis=1), wm_ref[...])
        xs.append(_layer_norm(ALPHA * x_ref[sl, :] + mix, gm_ref[...], bm_ref[...]))
    for h in range(2):
        _ffn_residual_ln(xs[h], wi_ref, wo_ref, g_ref, b_ref, o_ref, act_ref, row0=h * half)


def _mix_ffn_ln(x, yd, ys, w_mix, g_mix, b_mix, w_in, w_out, g, b, layer, tm):
    n = x.shape[0]
    row = lambda width: pl.BlockSpec((tm, width), lambda i: (i, 0))
    vec = _const_spec((1, D_MODEL))
    return pl.pallas_call(
        _mix_ffn_kernel,
        grid=(n // tm,),
        in_specs=[row(D_MODEL), row(DV), row(SQ), _layer_spec((DV + SQ, D_MODEL), layer), vec, vec,
                  _layer_spec((D_MODEL, 2 * D_FF), layer), _layer_spec((D_FF, D_MODEL), layer),
                  vec, vec],
        out_specs=row(D_MODEL),
        out_shape=jax.ShapeDtypeStruct((n, D_MODEL), F32),
        scratch_shapes=[pltpu.VMEM((tm, D_FF), BF16)],
        compiler_params=pltpu.CompilerParams(
            dimension_semantics=("parallel",), vmem_limit_bytes=VMEM_LIMIT),
        name="mix_ffn_ln",
    )(x, yd, ys, w_mix, g_mix, b_mix, w_in, w_out, g, b)


def _proj_kernel(x_ref, w_ref, cos_ref, sin_ref, ones_ref,
                 qd_ref, kd_ref, vd_ref, qs_ref, ks_ref, vs_ref, norm_ref, *, tm):
    xb = x_ref[...].astype(BF16)
    cos = cos_ref[...]
    sin = sin_ref[...]

    def rope(t, scale):
        r = t * cos + pltpu.roll(t, LANES // 2, 1) * sin
        return r * scale if scale != 1.0 else r

    def put_transposed(ref, col, t):
        biggest = None
        for blk in range(tm // BLOCK):
            tt = t[blk * BLOCK:(blk + 1) * BLOCK, :].T
            ref[0, blk, col:col + LANES, :] = tt.astype(BF16)
            n2 = jnp.sum(tt * tt, axis=0, keepdims=True)
            biggest = n2 if biggest is None else jnp.maximum(biggest, n2)
        return biggest

    def put_rows(ref, col, t):
        ref[:, col:col + LANES] = t.astype(BF16)

    plan = []
    for j in range(DQ // LANES):
        plan.append((j * LANES, qd_ref, j * LANES, put_transposed, Q_SCALE))
    for j in range(DK // LANES):
        plan.append((DQ + j * LANES, kd_ref, j * LANES, put_rows, 1.0))
    for j in range(SQ // LANES):
        plan.append((DQ + DK + j * LANES, qs_ref, j * LANES, put_transposed, Q_SCALE))
    plan.append((DQ + DK + SQ, ks_ref, 0, put_rows, 1.0))
    for j in range(DV // LANES):
        plan.append((ROPE_COLS + j * LANES, vd_ref, j * LANES, put_transposed, None))
    plan.append((ROPE_COLS + DV, vs_ref, 0, put_transposed, None))

    lane = lax.broadcasted_iota(jnp.int32, (1, LANES), 1)
    q_norms = jnp.zeros((1, LANES), F32)
    k_squares = []

    for first in range(0, len(plan), 2):
        lo = plan[first][0]
        t2 = _dot(xb, w_ref[:, lo:lo + 2 * LANES])
        for half, (_, ref, col, put, scale) in enumerate(plan[first:first + 2]):
            t = t2[:, half * LANES:(half + 1) * LANES]
            r = t if scale is None else rope(t, scale)
            per_token = put(ref, col, r)
            if ref is qd_ref or ref is qs_ref:
                slot = col // LANES + (0 if ref is qd_ref else 2 * DIFF_HEADS)
                q_norms = jnp.where(lane == slot, jnp.max(per_token, axis=1, keepdims=True), q_norms)
            elif ref is kd_ref or ref is ks_ref:
                k_squares.append((r * r).astype(BF16))

    k_norms = _dot(jnp.concatenate(k_squares, axis=1), ones_ref[...])
    norms = jnp.maximum(jnp.max(k_norms, axis=0, keepdims=True), q_norms)
    norm_ref[...] = jnp.broadcast_to(norms, (SUBLANES, LANES))[None]


def _in_proj(x, w, cos, sin, layer, batch, seq, tm):
    n = x.shape[0]
    nq = seq // BLOCK
    tpb = seq // tm
    sub = tm // BLOCK
    row = lambda width: pl.BlockSpec((tm, width), lambda i: (i, 0))
    tblk = lambda rows: pl.BlockSpec((1, sub, rows, BLOCK), lambda i: (i // tpb, i % tpb, 0, 0))
    tshape = lambda rows: jax.ShapeDtypeStruct((batch, nq, rows, BLOCK), BF16)
    k_lanes = jnp.asarray(list(range(DIFF_HEADS, 2 * DIFF_HEADS)) + [NORM_GROUPS - 1])
    group_ones = (jnp.repeat(k_lanes, LANES)[:, None] == jnp.arange(LANES)[None, :]).astype(BF16)
    *proj, norms = pl.pallas_call(
        functools.partial(_proj_kernel, tm=tm),
        grid=(n // tm,),
        in_specs=[row(D_MODEL), _layer_spec((D_MODEL, IN_COLS), layer), row(LANES), row(LANES),
                  _const_spec(((DIFF_HEADS + 1) * LANES, LANES))],
        out_specs=[tblk(DQ), row(DK), tblk(DV), tblk(SQ), row(SK), tblk(SV),
                   pl.BlockSpec((1, SUBLANES, LANES), lambda i: (i, 0, 0))],
        out_shape=[tshape(DQ), jax.ShapeDtypeStruct((n, DK), BF16), tshape(DV),
                   tshape(SQ), jax.ShapeDtypeStruct((n, SK), BF16), tshape(SV),
                   jax.ShapeDtypeStruct((n // tm, SUBLANES, LANES), F32)],
        compiler_params=pltpu.CompilerParams(
            dimension_semantics=("parallel",), vmem_limit_bytes=VMEM_LIMIT),
        name="in_proj_rope",
    )(x, w, cos, sin, group_ones)
    biggest = norms[:, 0, :NORM_GROUPS].reshape(batch, tpb, NORM_GROUPS).max(axis=1)
    q2, k2 = biggest[:, :DIFF_HEADS], biggest[:, DIFF_HEADS:2 * DIFF_HEADS]
    sq2, sk2 = biggest[:, 2 * DIFF_HEADS:NORM_GROUPS - 1].max(axis=1), biggest[:, NORM_GROUPS - 1]
    diff_bound = jnp.sqrt(q2 * k2) * BF16_ROUNDING_MARGIN
    swa_bound = jnp.sqrt(sq2 * sk2) * BF16_ROUNDING_MARGIN
    return proj, diff_bound, swa_bound


def _head_select_mask():
    r = lax.broadcasted_iota(jnp.int32, (LANES, 1), 0)
    return (r % HEAD_DIM) < HALF


def _pair_queries(qt, sel0):
    zero = jnp.zeros_like(qt)
    return jnp.concatenate([jnp.where(sel0, qt, zero), jnp.where(sel0, zero, qt)], axis=1)


DIFF_KV_TILE = 256
DIFF_SCORE_BUFFERS = 4
DIFF_BLOCKS_PER_STEP = 4
DIRECT_BLOCKS_PER_STEP = 16
DIRECT_LOOKAHEAD = 6


def _diff_attn_kernel(bound_ref, q_ref, k_ref, v_ref, lam_ref, prm_ref, y_ref, *s_refs, seq):
    lv = lam_ref[...]
    a1 = jnp.sum(lv[0:1, :] * lv[1:2, :], axis=-1, keepdims=True)
    a2 = jnp.sum(lv[2:3, :] * lv[3:4, :], axis=-1, keepdims=True)
    lam = jnp.exp(a1) - jnp.exp(a2) + prm_ref[2:3, 0:1]
    gain = prm_ref[0:1, :]
    one_minus_init = prm_ref[1:2, :]
    sel0 = _head_select_mask()
    tk = min(DIFF_KV_TILE, seq)
    nq = seq // BLOCK
    tiles = [(j * tk, (j + 1) * tk) for j in range(seq // tk)]

    def scores(i, s_ref):
        qm = _pair_queries(q_ref[0, i], sel0)
        m8 = None
        for lo, hi in tiles:
            s = _dot(k_ref[lo:hi, :], qm)
            s_ref[lo:hi, :] = s
            tile_max = jnp.max(s.reshape(tk // SUBLANES, SUBLANES, 2 * BLOCK), axis=0)
            m8 = tile_max if m8 is None else jnp.maximum(m8, tile_max)
        return jnp.max(m8, axis=0, keepdims=True)

    def weighted_values(p, lo, hi, o, l8):
        l8 = l8 + jnp.sum(p.reshape(tk // SUBLANES, SUBLANES, 2 * BLOCK), axis=0)
        vt = jnp.concatenate([v_ref[0, t] for t in range(lo // BLOCK, hi // BLOCK)], axis=1)
        return o + _dot(vt, p.astype(BF16)), l8

    def finish(i, s_ref, m):
        l8 = jnp.zeros((SUBLANES, 2 * BLOCK), F32)
        o = jnp.zeros((LANES, 2 * BLOCK), F32)
        for lo, hi in tiles:
            o, l8 = weighted_values(jnp.exp2(s_ref[lo:hi, :] - m), lo, hi, o, l8)
        write_block(i, o, l8)

    def write_block(i, o, l8):
        on = o * (1.0 / jnp.sum(l8, axis=0, keepdims=True))
        d = (on[:, :BLOCK] - lam * on[:, BLOCK:]).T
        ms = jnp.mean(d * d, axis=-1, keepdims=True)
        y = d * lax.rsqrt(ms + RMS_EPS) * gain * one_minus_init
        start = i * BLOCK if isinstance(i, int) else pl.multiple_of(i * BLOCK, BLOCK)
        y_ref[pl.ds(start, BLOCK), :] = y.astype(BF16)

    nbuf = len(s_refs)

    def shifted_path():
        group = DIFF_BLOCKS_PER_STEP if nq % DIFF_BLOCKS_PER_STEP == 0 else nbuf
        steps = nq // group

        def block_group(t, m, last=False):
            for k in range(group):
                i = group * t + k
                if not (last and k == group - 1):
                    m_next = scores(i + 1, s_refs[(k + 1) % nbuf])
                finish(i, s_refs[k % nbuf], m)
                m = m_next
            return m

        m = lax.fori_loop(0, steps - 1, block_group, scores(0, s_refs[0]))
        block_group(steps - 1, m, last=True)

    def direct_path():
        group = DIRECT_BLOCKS_PER_STEP if nq % DIRECT_BLOCKS_PER_STEP == 0 else nbuf
        steps = nq // group

        def direct_group(t, carry):
            blocks = [group * t + k for k in range(group)]
            qms = [None] * group
            acc = [None] * group
            pending = []

            def consume():
                k, lo, hi, s = pending.pop(0)
                o, l8 = acc[k] if acc[k] is not None else (
                    jnp.zeros((LANES, 2 * BLOCK), F32), jnp.zeros((SUBLANES, 2 * BLOCK), F32))
                acc[k] = weighted_values(jnp.exp2(s), lo, hi, o, l8)
                if hi == seq:
                    write_block(blocks[k], *acc[k])

            for k in range(group):
                qms[k] = _pair_queries(q_ref[0, blocks[k]], sel0)
                for lo, hi in tiles:
                    pending.append((k, lo, hi, _dot(k_ref[lo:hi, :], qms[k])))
                    if len(pending) > DIRECT_LOOKAHEAD:
                        consume()
            while pending:
                consume()
            return carry
        lax.fori_loop(0, steps, direct_group, 0)

    bound = bound_ref[pl.program_id(0), pl.program_id(1)]
    lax.cond(bound < DIRECT_SCORE_LIMIT, direct_path, shifted_path)


def _diff_attn(bound, qd, kd, vd, lam_vec, prm, batch, seq):
    nq = seq // BLOCK
    tspec = pl.BlockSpec((1, nq, LANES, BLOCK), lambda b, h: (b, 0, h, 0))
    kspec = pl.BlockSpec((seq, LANES), lambda b, h: (b, h))
    small = lambda shape: pl.BlockSpec(shape, lambda b, h: (0, 0))
    return pl.pallas_call(
        functools.partial(_diff_attn_kernel, seq=seq),
        grid=(batch, DIFF_HEADS),
        in_specs=[pl.BlockSpec(memory_space=pltpu.SMEM),
                  tspec, kspec, tspec, small((4, HEAD_DIM)), small((SUBLANES, LANES))],
        out_specs=pl.BlockSpec((seq, LANES), lambda b, h: (b, h)),
        out_shape=jax.ShapeDtypeStruct((batch * seq, DV), BF16),
        scratch_shapes=[pltpu.VMEM((seq, 2 * BLOCK), F32)] * DIFF_SCORE_BUFFERS,
        compiler_params=pltpu.CompilerParams(
            dimension_semantics=("parallel", "parallel"), vmem_limit_bytes=VMEM_LIMIT),
        name="diff_attn",
    )(bound, qd, kd, vd, lam_vec, prm)


SWA_BLOCKS_PER_STEP = 8
SWA_LOOKAHEAD = 6


def _swa_kernel(bound_ref, q_ref, k_ref, v_ref, sink_ref, y_ref, *, seq):
    nq = seq // BLOCK
    sel0 = _head_select_mask()
    r = lax.broadcasted_iota(jnp.int32, (BLOCK, BLOCK), 0)
    c = lax.broadcasted_iota(jnp.int32, (BLOCK, BLOCK), 1)
    prev_band = jnp.where(c <= r, 0.0, NEG).astype(F32)
    next_band = jnp.where(r <= c, 0.0, NEG).astype(F32)

    rows = lambda blk: pl.ds(pl.multiple_of(blk * BLOCK, BLOCK), BLOCK)
    unroll = SWA_BLOCKS_PER_STEP if nq % SWA_BLOCKS_PER_STEP == 0 else 1

    def load_block(n):
        n_prev = jnp.maximum(n - 1, 0)
        n_next = jnp.minimum(n + 1, nq - 1)
        edge_prev = jnp.where(n >= 1, 0.0, NEG).astype(F32)
        edge_next = jnp.where(n <= nq - 2, 0.0, NEG).astype(F32)
        bias_prev = jnp.concatenate([prev_band + edge_prev] * 2, axis=1)
        bias_next = jnp.concatenate([next_band + edge_next] * 2, axis=1)
        bias = (bias_prev, bias_next)
        kb = jnp.concatenate([k_ref[rows(n_prev), :], k_ref[rows(n), :], k_ref[rows(n_next), :]],
                             axis=0)
        vb = jnp.concatenate([v_ref[0, n_prev], v_ref[0, n], v_ref[0, n_next]], axis=1)
        qms = [_pair_queries(q_ref[0, n, g * LANES:(g + 1) * LANES, :], sel0)
               for g in range(SWA_GROUP)]
        return kb, vb, bias, qms

    def masked_scores(kb, qm, bias):
        s = _dot(kb, qm)
        return jnp.concatenate([s[:BLOCK] + bias[0], s[BLOCK:2 * BLOCK],
                                s[2 * BLOCK:] + bias[1]], axis=0)

    def step(t, carry, shifted):
        blocks = [unroll * t + u for u in range(unroll)]
        loaded = [load_block(n) for n in blocks]
        pending, outs = [], [[] for _ in blocks]

        def consume():
            u, vb, g, s = pending.pop(0)
            sink = sink_ref[:, g * 2 * BLOCK:(g + 1) * 2 * BLOCK]
            if shifted:
                m = jnp.maximum(jnp.max(s, axis=0, keepdims=True), sink)
                e = jnp.exp2(s - m)
                den = jnp.sum(e, axis=0, keepdims=True) + jnp.exp2(sink - m)
            else:
                e = jnp.exp2(s)
                den = jnp.sum(e, axis=0, keepdims=True) + jnp.exp2(sink)
            on = _dot(vb, e.astype(BF16)) * (1.0 / den)
            yt = jnp.concatenate([on[:HEAD_DIM, :BLOCK], on[HEAD_DIM:, BLOCK:]], axis=0)
            outs[u].append(yt.T.astype(BF16))
            if g == SWA_GROUP - 1:
                y_ref[rows(blocks[u]), :] = jnp.concatenate(outs[u], axis=1)

        for u, (kb, vb, bias, qms) in enumerate(loaded):
            for g, qm in enumerate(qms):
                pending.append((u, vb, g, masked_scores(kb, qm, bias)))
                if len(pending) > SWA_LOOKAHEAD:
                    consume()
        while pending:
            consume()
        return carry

    lax.cond(bound_ref[pl.program_id(0)] < DIRECT_SCORE_LIMIT,
             lambda: lax.fori_loop(0, nq // unroll, functools.partial(step, shifted=False), 0),
             lambda: lax.fori_loop(0, nq // unroll, functools.partial(step, shifted=True), 0))


def _swa_attn(bound, qs, ks, vs, sink_vec, batch, seq):
    nq = seq // BLOCK
    return pl.pallas_call(
        functools.partial(_swa_kernel, seq=seq),
        grid=(batch,),
        in_specs=[pl.BlockSpec(memory_space=pltpu.SMEM),
                  pl.BlockSpec((1, nq, SQ, BLOCK), lambda b: (b, 0, 0, 0)),
                  pl.BlockSpec((seq, SK), lambda b: (b, 0)),
                  pl.BlockSpec((1, nq, SV, BLOCK), lambda b: (b, 0, 0, 0)),
                  pl.BlockSpec((1, SWA_Q_HEADS * BLOCK), lambda b: (0, 0))],
        out_specs=pl.BlockSpec((seq, SQ), lambda b: (b, 0)),
        out_shape=jax.ShapeDtypeStruct((batch * seq, SQ), BF16),
        compiler_params=pltpu.CompilerParams(
            dimension_semantics=("parallel",), vmem_limit_bytes=VMEM_LIMIT),
        name="swa_attn",
    )(bound, qs, ks, vs, sink_vec)


def _token_tile(n, want):
    return want if n % want == 0 else BLOCK


def kernel(x, positions, w_in, w_out, diff_lambda, diff_subln_g, swa_sink,
           ffn1_w_in, ffn1_w_out, ffn2_w_in, ffn2_w_out, ln_g, ln_b):
    batch, seq, _ = x.shape
    n = batch * seq
    tm = _token_tile(seq, 1024)
    tm_ffn = _token_tile(n, FFN_TOKEN_TILE)
    h = x.reshape(n, D_MODEL)
    cos, sin = _rope_tables(positions)

    w_in_p = _permute_in_proj(w_in).astype(BF16)
    w_out_p = _permute_out_proj(w_out).astype(BF16)
    ffn1_in, ffn1_out = ffn1_w_in.astype(BF16), ffn1_w_out.astype(BF16)
    ffn2_in, ffn2_out = ffn2_w_in.astype(BF16), ffn2_w_out.astype(BF16)
    sink = swa_sink.reshape(DEPTH, SWA_KV_HEADS, SWA_GROUP).transpose(0, 2, 1) * LOG2E
    sink_vec = jnp.repeat(sink.reshape(DEPTH, 1, SWA_Q_HEADS), BLOCK, axis=2)

    for l in range(DEPTH):
        lambda_init = 0.8 - 0.6 * math.exp(-0.3 * l)
        g = lambda i: ln_g[l, i].reshape(1, D_MODEL)
        b = lambda i: ln_b[l, i].reshape(1, D_MODEL)
        prm = jnp.zeros((SUBLANES, LANES), F32)
        prm = prm.at[0].set(diff_subln_g[l]).at[1].set(1.0 - lambda_init).at[2].set(lambda_init)

        h = _ffn_ln(h, ffn1_in, ffn1_out, g(0), b(0), l, tm_ffn)
        (qd, kd, vd, qs, ks, vs), diff_bound, swa_bound = _in_proj(
            h, w_in_p, cos, sin, l, batch, seq, tm)
        swa_bound = jnp.maximum(swa_bound, jnp.max(jnp.abs(sink_vec[l])))
        yd = _diff_attn(diff_bound, qd, kd, vd, diff_lambda[l], prm, batch, seq)
        ys = _swa_attn(swa_bound, qs, ks, vs, sink_vec[l], batch, seq)
        h = _mix_ffn_ln(h, yd, ys, w_out_p, g(1), b(1), ffn2_in, ffn2_out, g(2), b(2), l, tm_ffn)
    return h.reshape(batch, seq, D_MODEL)
```

```python
import functools
import math

import jax
import jax.numpy as jnp
from jax import lax
from jax.experimental import pallas as pl
from jax.experimental.pallas import tpu as pltpu

D_MODEL = 1024
DEPTH = 4
HEAD_DIM = 64
HALF = HEAD_DIM // 2
DIFF_HEADS = 4
SWA_Q_HEADS = 8
SWA_KV_HEADS = 2
SWA_GROUP = SWA_Q_HEADS // SWA_KV_HEADS
BLOCK = 128
D_FF = 2816
ROPE_THETA = 10000.0
ALPHA = (2.0 * DEPTH) ** 0.25
LN_EPS = 1e-5
RMS_EPS = 1e-5
LOG2E = 1.4426950408889634
Q_SCALE = (HEAD_DIM ** -0.5) * LOG2E
NEG = -1e30

LANES = 128
SUBLANES = 8
DQ = DK = DV = 512
SQ = 512
SK = SV = 128
IN_COLS = DQ + DK + DV + SQ + SK + SV
ROPE_COLS = DQ + DK + SQ + SK

VMEM_LIMIT = 56 * 1024 * 1024

NORM_GROUPS = 2 * DIFF_HEADS + SWA_GROUP + 1
DIRECT_SCORE_LIMIT = 64.0
BF16_ROUNDING_MARGIN = 1.02

BF16 = jnp.bfloat16
F32 = jnp.float32


def _permute_in_proj(w):
    lead = w.shape[:-1]
    dq, dk, dv = w[..., :DQ], w[..., DQ:DQ + DK], w[..., DQ + DK:DQ + DK + DV]
    sq0 = DQ + DK + DV
    sq, sk, sv = w[..., sq0:sq0 + SQ], w[..., sq0 + SQ:sq0 + SQ + SK], w[..., sq0 + SQ + SK:]
    pair = lambda t: t.reshape(lead + (DIFF_HEADS, 2, 2, HALF)).swapaxes(-3, -2).reshape(lead + (DQ,))
    sq = jnp.moveaxis(sq.reshape(lead + (SWA_KV_HEADS, SWA_GROUP, 2, HALF)), -4, -2)
    sk = sk.reshape(lead + (SWA_KV_HEADS, 2, HALF)).swapaxes(-3, -2)
    return jnp.concatenate(
        [pair(dq), pair(dk), sq.reshape(lead + (SQ,)), sk.reshape(lead + (SK,)), dv, sv], axis=-1)


def _permute_out_proj(w):
    lead, tail = w.shape[:-2], w.shape[-1:]
    swa = w[..., DV:, :].reshape(lead + (SWA_KV_HEADS, SWA_GROUP, HEAD_DIM) + tail)
    swa = swa.swapaxes(-4, -3).reshape(lead + (SQ,) + tail)
    return jnp.concatenate([w[..., :DV, :], swa], axis=-2)


def _layer_norm(z, g, b):
    mu = jnp.mean(z, axis=-1, keepdims=True)
    zc = z - mu
    var = jnp.mean(zc * zc, axis=-1, keepdims=True)
    return zc * lax.rsqrt(var + LN_EPS) * g + b


def _dot(a, b):
    return jnp.dot(a, b, preferred_element_type=F32)


ROPE_PACK = LANES // HALF
ROPE_TILE = 2048


def _rope_table_kernel(pos_ref, inv_ref, sign_ref, cos_ref, sin_ref):
    rows = pos_ref.shape[0]
    ang = pos_ref[...] * inv_ref[...]
    cos = jnp.cos(ang)
    sin = jnp.sin(ang)
    first_group = lax.broadcasted_iota(jnp.int32, (rows, LANES), 1) < HALF

    def spread(t, j):
        x = t if j == 0 else pltpu.roll(t, LANES - HALF * j, 1)
        x = jnp.where(first_group, x, 0.0)
        x = x + pltpu.roll(x, HALF, 1)
        return x + pltpu.roll(x, 2 * HALF, 1)

    for j in range(ROPE_PACK):
        cos_ref[j * rows:(j + 1) * rows, :] = spread(cos, j)
        sin_ref[j * rows:(j + 1) * rows, :] = spread(sin, j) * sign_ref[...]


def _rope_tables(positions):
    n = positions.size
    tile = min(n, ROPE_TILE)
    rows = tile // ROPE_PACK
    inv = jnp.power(ROPE_THETA, -jnp.arange(0, HEAD_DIM, 2, dtype=F32) / HEAD_DIM)
    inv128 = jnp.tile(inv, ROPE_PACK).reshape(1, LANES)
    sign = jnp.where(jnp.arange(LANES) < LANES // 2, -1.0, 1.0).astype(F32).reshape(1, LANES)
    pos = positions.reshape(n // tile, ROPE_PACK, rows).astype(F32).transpose(0, 2, 1)
    pos = jnp.repeat(pos, HALF, axis=2).reshape(n // ROPE_PACK, LANES)
    packed = pl.BlockSpec((rows, LANES), lambda i: (i, 0))
    row = pl.BlockSpec((tile, LANES), lambda i: (i, 0))
    vec = pl.BlockSpec((1, LANES), lambda i: (0, 0))
    return pl.pallas_call(
        _rope_table_kernel,
        grid=(n // tile,),
        in_specs=[packed, vec, vec],
        out_specs=[row, row],
        out_shape=[jax.ShapeDtypeStruct((n, LANES), F32)] * 2,
        name="rope_tables",
    )(pos, inv128, sign)


FFN_CHUNK = 256
FFN_TOKEN_TILE = 1024
MIX_ROW_GROUPS = 4
FFN_OUT_ROWS = 256


def _ffn_residual_ln(x, wi_ref, wo_ref, g_ref, b_ref, o_ref, act_ref, row0=0):
    xb = x.astype(BF16)
    n_rows = x.shape[0]
    for c in range(D_FF // FFN_CHUNK):
        lo = c * FFN_CHUNK
        gate = _dot(xb, wi_ref[:, lo:lo + FFN_CHUNK])
        up = _dot(xb, wi_ref[:, D_FF + lo:D_FF + lo + FFN_CHUNK])
        act = gate * (1.0 / (1.0 + jnp.exp(-gate))) * up
        act_ref[row0:row0 + n_rows, lo:lo + FFN_CHUNK] = act.astype(BF16)
    rows = min(FFN_OUT_ROWS, n_rows)
    for r in range(n_rows // rows):
        sl = slice(r * rows, (r + 1) * rows)
        out = slice(row0 + r * rows, row0 + (r + 1) * rows)
        y = _dot(act_ref[out, :], wo_ref[...])
        o_ref[out, :] = _layer_norm(ALPHA * x[sl, :] + 0.5 * y, g_ref[...], b_ref[...])


def _ffn_kernel(x_ref, wi_ref, wo_ref, g_ref, b_ref, o_ref, act_ref):
    _ffn_residual_ln(x_ref[...], wi_ref, wo_ref, g_ref, b_ref, o_ref, act_ref)


def _const_spec(shape):
    return pl.BlockSpec(shape, lambda i: (0, 0), pipeline_mode=pl.Buffered(1))


def _layer_spec(shape, layer):
    return pl.BlockSpec((None,) + shape, lambda i: (layer, 0, 0), pipeline_mode=pl.Buffered(1))


def _ffn_ln(x, w_in, w_out, g, b, layer, tm):
    n = x.shape[0]
    row = pl.BlockSpec((tm, D_MODEL), lambda i: (i, 0))
    return pl.pallas_call(
        _ffn_kernel,
        grid=(n // tm,),
        in_specs=[row, _layer_spec((D_MODEL, 2 * D_FF), layer), _layer_spec((D_FF, D_MODEL), layer),
                  _const_spec((1, D_MODEL)), _const_spec((1, D_MODEL))],
        out_specs=row,
        out_shape=jax.ShapeDtypeStruct((n, D_MODEL), F32),
        scratch_shapes=[pltpu.VMEM((tm, D_FF), BF16)],
        compiler_params=pltpu.CompilerParams(
            dimension_semantics=("parallel",), vmem_limit_bytes=VMEM_LIMIT),
        name="ffn_ln",
    )(x, w_in, w_out, g, b)


def _mix_ffn_kernel(x_ref, yd_ref, ys_ref, wm_ref, gm_ref, bm_ref,
                    wi_ref, wo_ref, g_ref, b_ref, o_ref, act_ref):
    part = x_ref.shape[0] // MIX_ROW_GROUPS
    xs = []
    for h in range(MIX_ROW_GROUPS):
        sl = slice(h * part, (h + 1) * part)
        mix = _dot(jnp.concatenate([yd_ref[sl, :], ys_ref[sl, :]], axis=1), wm_ref[...])
        xs.append(_layer_norm(ALPHA * x_ref[sl, :] + mix, gm_ref[...], bm_ref[...]))
    for h in range(MIX_ROW_GROUPS):
        _ffn_residual_ln(xs[h], wi_ref, wo_ref, g_ref, b_ref, o_ref, act_ref, row0=h * part)


def _mix_ffn_ln(x, yd, ys, w_mix, g_mix, b_mix, w_in, w_out, g, b, layer, tm):
    n = x.shape[0]
    row = lambda width: pl.BlockSpec((tm, width), lambda i: (i, 0))
    vec = _const_spec((1, D_MODEL))
    return pl.pallas_call(
        _mix_ffn_kernel,
        grid=(n // tm,),
        in_specs=[row(D_MODEL), row(DV), row(SQ), _layer_spec((DV + SQ, D_MODEL), layer), vec, vec,
                  _layer_spec((D_MODEL, 2 * D_FF), layer), _layer_spec((D_FF, D_MODEL), layer),
                  vec, vec],
        out_specs=row(D_MODEL),
        out_shape=jax.ShapeDtypeStruct((n, D_MODEL), F32),
        scratch_shapes=[pltpu.VMEM((tm, D_FF), BF16)],
        compiler_params=pltpu.CompilerParams(
            dimension_semantics=("parallel",), vmem_limit_bytes=VMEM_LIMIT),
        name="mix_ffn_ln",
    )(x, yd, ys, w_mix, g_mix, b_mix, w_in, w_out, g, b)


def _proj_kernel(x_ref, w_ref, cos_ref, sin_ref, ones_ref,
                 qd_ref, kd_ref, vd_ref, qs_ref, ks_ref, vs_ref, norm_ref, *, tm):
    xb = x_ref[...].astype(BF16)
    cos = cos_ref[...]
    sin = sin_ref[...]

    def rope(t, scale):
        r = t * cos + pltpu.roll(t, LANES // 2, 1) * sin
        return r * scale if scale != 1.0 else r

    def put_transposed(ref, col, t):
        biggest = None
        for blk in range(tm // BLOCK):
            tt = t[blk * BLOCK:(blk + 1) * BLOCK, :].T
            ref[0, blk, col:col + LANES, :] = tt.astype(BF16)
            n2 = jnp.sum(tt * tt, axis=0, keepdims=True)
            biggest = n2 if biggest is None else jnp.maximum(biggest, n2)
        return biggest

    def put_rows(ref, col, t):
        ref[:, col:col + LANES] = t.astype(BF16)

    plan = []
    for j in range(DQ // LANES):
        plan.append((j * LANES, qd_ref, j * LANES, put_transposed, Q_SCALE))
    for j in range(DK // LANES):
        plan.append((DQ + j * LANES, kd_ref, j * LANES, put_rows, 1.0))
    for j in range(SQ // LANES):
        plan.append((DQ + DK + j * LANES, qs_ref, j * LANES, put_transposed, Q_SCALE))
    plan.append((DQ + DK + SQ, ks_ref, 0, put_rows, 1.0))
    for j in range(DV // LANES):
        plan.append((ROPE_COLS + j * LANES, vd_ref, j * LANES, put_transposed, None))
    plan.append((ROPE_COLS + DV, vs_ref, 0, put_transposed, None))

    lane = lax.broadcasted_iota(jnp.int32, (1, LANES), 1)
    q_norms = jnp.zeros((1, LANES), F32)
    k_squares = []

    for first in range(0, len(plan), 2):
        lo = plan[first][0]
        t2 = _dot(xb, w_ref[:, lo:lo + 2 * LANES])
        for half, (_, ref, col, put, scale) in enumerate(plan[first:first + 2]):
            t = t2[:, half * LANES:(half + 1) * LANES]
            r = t if scale is None else rope(t, scale)
            per_token = put(ref, col, r)
            if ref is qd_ref or ref is qs_ref:
                slot = col // LANES + (0 if ref is qd_ref else 2 * DIFF_HEADS)
                q_norms = jnp.where(lane == slot, jnp.max(per_token, axis=1, keepdims=True), q_norms)
            elif ref is kd_ref or ref is ks_ref:
                k_squares.append((r * r).astype(BF16))

    k_norms = _dot(jnp.concatenate(k_squares, axis=1), ones_ref[...])
    norms = jnp.maximum(jnp.max(k_norms, axis=0, keepdims=True), q_norms)
    norm_ref[...] = jnp.broadcast_to(norms, (SUBLANES, LANES))[None]


def _in_proj(x, w, cos, sin, layer, batch, seq, tm):
    n = x.shape[0]
    nq = seq // BLOCK
    tpb = seq // tm
    sub = tm // BLOCK
    row = lambda width: pl.BlockSpec((tm, width), lambda i: (i, 0))
    tblk = lambda rows: pl.BlockSpec((1, sub, rows, BLOCK), lambda i: (i // tpb, i % tpb, 0, 0))
    tshape = lambda rows: jax.ShapeDtypeStruct((batch, nq, rows, BLOCK), BF16)
    k_lanes = jnp.asarray(list(range(DIFF_HEADS, 2 * DIFF_HEADS)) + [NORM_GROUPS - 1])
    group_ones = (jnp.repeat(k_lanes, LANES)[:, None] == jnp.arange(LANES)[None, :]).astype(BF16)
    *proj, norms = pl.pallas_call(
        functools.partial(_proj_kernel, tm=tm),
        grid=(n // tm,),
        in_specs=[row(D_MODEL), _layer_spec((D_MODEL, IN_COLS), layer), row(LANES), row(LANES),
                  _const_spec(((DIFF_HEADS + 1) * LANES, LANES))],
        out_specs=[tblk(DQ), row(DK), tblk(DV), tblk(SQ), row(SK), tblk(SV),
                   pl.BlockSpec((1, SUBLANES, LANES), lambda i: (i, 0, 0))],
        out_shape=[tshape(DQ), jax.ShapeDtypeStruct((n, DK), BF16), tshape(DV),
                   tshape(SQ), jax.ShapeDtypeStruct((n, SK), BF16), tshape(SV),
                   jax.ShapeDtypeStruct((n // tm, SUBLANES, LANES), F32)],
        compiler_params=pltpu.CompilerParams(
            dimension_semantics=("parallel",), vmem_limit_bytes=VMEM_LIMIT),
        name="in_proj_rope",
    )(x, w, cos, sin, group_ones)
    biggest = norms[:, 0, :NORM_GROUPS].reshape(batch, tpb, NORM_GROUPS).max(axis=1)
    q2, k2 = biggest[:, :DIFF_HEADS], biggest[:, DIFF_HEADS:2 * DIFF_HEADS]
    sq2, sk2 = biggest[:, 2 * DIFF_HEADS:NORM_GROUPS - 1].max(axis=1), biggest[:, NORM_GROUPS - 1]
    diff_bound = jnp.sqrt(q2 * k2) * BF16_ROUNDING_MARGIN
    swa_bound = jnp.sqrt(sq2 * sk2) * BF16_ROUNDING_MARGIN
    return proj, diff_bound, swa_bound


def _head_select_mask():
    r = lax.broadcasted_iota(jnp.int32, (LANES, 1), 0)
    return (r % HEAD_DIM) < HALF


def _pair_queries(qt, sel0):
    zero = jnp.zeros_like(qt)
    return jnp.concatenate([jnp.where(sel0, qt, zero), jnp.where(sel0, zero, qt)], axis=1)


DIFF_KV_TILE = 256
DIFF_SCORE_BUFFERS = 4
DIFF_BLOCKS_PER_STEP = 4
DIRECT_BLOCKS_PER_STEP = 16
DIRECT_LOOKAHEAD = 6


def _diff_attn_kernel(bound_ref, q_ref, k_ref, v_ref, lam_ref, prm_ref, y_ref, *s_refs, seq):
    lv = lam_ref[...]
    a1 = jnp.sum(lv[0:1, :] * lv[1:2, :], axis=-1, keepdims=True)
    a2 = jnp.sum(lv[2:3, :] * lv[3:4, :], axis=-1, keepdims=True)
    lam = jnp.exp(a1) - jnp.exp(a2) + prm_ref[2:3, 0:1]
    gain = prm_ref[0:1, :]
    one_minus_init = prm_ref[1:2, :]
    sel0 = _head_select_mask()
    tk = min(DIFF_KV_TILE, seq)
    nq = seq // BLOCK
    tiles = [(j * tk, (j + 1) * tk) for j in range(seq // tk)]

    def scores(i, s_ref):
        qm = _pair_queries(q_ref[0, i], sel0)
        m8 = None
        for lo, hi in tiles:
            s = _dot(k_ref[lo:hi, :], qm)
            s_ref[lo:hi, :] = s
            tile_max = jnp.max(s.reshape(tk // SUBLANES, SUBLANES, 2 * BLOCK), axis=0)
            m8 = tile_max if m8 is None else jnp.maximum(m8, tile_max)
        return jnp.max(m8, axis=0, keepdims=True)

    def weighted_values(p, lo, hi, o, l8):
        l8 = l8 + jnp.sum(p.reshape(tk // SUBLANES, SUBLANES, 2 * BLOCK), axis=0)
        vt = jnp.concatenate([v_ref[0, t] for t in range(lo // BLOCK, hi // BLOCK)], axis=1)
        return o + _dot(vt, p.astype(BF16)), l8

    def finish(i, s_ref, m):
        l8 = jnp.zeros((SUBLANES, 2 * BLOCK), F32)
        o = jnp.zeros((LANES, 2 * BLOCK), F32)
        for lo, hi in tiles:
            o, l8 = weighted_values(jnp.exp2(s_ref[lo:hi, :] - m), lo, hi, o, l8)
        write_block(i, o, l8)

    def write_block(i, o, l8):
        on = o * (1.0 / jnp.sum(l8, axis=0, keepdims=True))
        d = (on[:, :BLOCK] - lam * on[:, BLOCK:]).T
        ms = jnp.mean(d * d, axis=-1, keepdims=True)
        y = d * lax.rsqrt(ms + RMS_EPS) * gain * one_minus_init
        start = i * BLOCK if isinstance(i, int) else pl.multiple_of(i * BLOCK, BLOCK)
        y_ref[pl.ds(start, BLOCK), :] = y.astype(BF16)

    nbuf = len(s_refs)

    def shifted_path():
        group = DIFF_BLOCKS_PER_STEP if nq % DIFF_BLOCKS_PER_STEP == 0 else nbuf
        steps = nq // group

        def block_group(t, m, last=False):
            for k in range(group):
                i = group * t + k
                if not (last and k == group - 1):
                    m_next = scores(i + 1, s_refs[(k + 1) % nbuf])
                finish(i, s_refs[k % nbuf], m)
                m = m_next
            return m

        m = lax.fori_loop(0, steps - 1, block_group, scores(0, s_refs[0]))
        block_group(steps - 1, m, last=True)

    def direct_path():
        group = DIRECT_BLOCKS_PER_STEP if nq % DIRECT_BLOCKS_PER_STEP == 0 else nbuf
        steps = nq // group

        def direct_group(t, carry):
            blocks = [group * t + k for k in range(group)]
            qms = [None] * group
            acc = [None] * group
            pending = []

            def consume():
                k, lo, hi, s = pending.pop(0)
                o, l8 = acc[k] if acc[k] is not None else (
                    jnp.zeros((LANES, 2 * BLOCK), F32), jnp.zeros((SUBLANES, 2 * BLOCK), F32))
                acc[k] = weighted_values(jnp.exp2(s), lo, hi, o, l8)
                if hi == seq:
                    write_block(blocks[k], *acc[k])

            for k in range(group):
                qms[k] = _pair_queries(q_ref[0, blocks[k]], sel0)
                for lo, hi in tiles:
                    pending.append((k, lo, hi, _dot(k_ref[lo:hi, :], qms[k])))
                    if len(pending) > DIRECT_LOOKAHEAD:
                        consume()
            while pending:
                consume()
            return carry
        lax.fori_loop(0, steps, direct_group, 0)

    bound = bound_ref[pl.program_id(0), pl.program_id(1)]
    lax.cond(bound < DIRECT_SCORE_LIMIT, direct_path, shifted_path)


def _diff_attn(bound, qd, kd, vd, lam_vec, prm, batch, seq):
    nq = seq // BLOCK
    tspec = pl.BlockSpec((1, nq, LANES, BLOCK), lambda b, h: (b, 0, h, 0))
    kspec = pl.BlockSpec((seq, LANES), lambda b, h: (b, h))
    small = lambda shape: pl.BlockSpec(shape, lambda b, h: (0, 0))
    return pl.pallas_call(
        functools.partial(_diff_attn_kernel, seq=seq),
        grid=(batch, DIFF_HEADS),
        in_specs=[pl.BlockSpec(memory_space=pltpu.SMEM),
                  tspec, kspec, tspec, small((4, HEAD_DIM)), small((SUBLANES, LANES))],
        out_specs=pl.BlockSpec((seq, LANES), lambda b, h: (b, h)),
        out_shape=jax.ShapeDtypeStruct((batch * seq, DV), BF16),
        scratch_shapes=[pltpu.VMEM((seq, 2 * BLOCK), F32)] * DIFF_SCORE_BUFFERS,
        compiler_params=pltpu.CompilerParams(
            dimension_semantics=("parallel", "parallel"), vmem_limit_bytes=VMEM_LIMIT),
        name="diff_attn",
    )(bound, qd, kd, vd, lam_vec, prm)


SWA_BLOCKS_PER_STEP = 8
SWA_LOOKAHEAD = 6


def _swa_kernel(bound_ref, q_ref, k_ref, v_ref, sink_ref, y_ref, *, seq):
    nq = seq // BLOCK
    sel0 = _head_select_mask()
    r = lax.broadcasted_iota(jnp.int32, (BLOCK, BLOCK), 0)
    c = lax.broadcasted_iota(jnp.int32, (BLOCK, BLOCK), 1)
    prev_band = jnp.where(c <= r, 0.0, NEG).astype(F32)
    next_band = jnp.where(r <= c, 0.0, NEG).astype(F32)

    rows = lambda blk: pl.ds(pl.multiple_of(blk * BLOCK, BLOCK), BLOCK)
    unroll = SWA_BLOCKS_PER_STEP if nq % SWA_BLOCKS_PER_STEP == 0 else 1

    def load_block(n):
        n_prev = jnp.maximum(n - 1, 0)
        n_next = jnp.minimum(n + 1, nq - 1)
        edge_prev = jnp.where(n >= 1, 0.0, NEG).astype(F32)
        edge_next = jnp.where(n <= nq - 2, 0.0, NEG).astype(F32)
        bias_prev = jnp.concatenate([prev_band + edge_prev] * 2, axis=1)
        bias_next = jnp.concatenate([next_band + edge_next] * 2, axis=1)
        bias = (bias_prev, bias_next)
        kb = jnp.concatenate([k_ref[rows(n_prev), :], k_ref[rows(n), :], k_ref[rows(n_next), :]],
                             axis=0)
        vb = jnp.concatenate([v_ref[0, n_prev], v_ref[0, n], v_ref[0, n_next]], axis=1)
        qms = [_pair_queries(q_ref[0, n, g * LANES:(g + 1) * LANES, :], sel0)
               for g in range(SWA_GROUP)]
        return kb, vb, bias, qms

    def masked_scores(kb, qm, bias):
        s = _dot(kb, qm)
        return jnp.concatenate([s[:BLOCK] + bias[0], s[BLOCK:2 * BLOCK],
                                s[2 * BLOCK:] + bias[1]], axis=0)

    def step(t, carry, shifted):
        blocks = [unroll * t + u for u in range(unroll)]
        loaded = [load_block(n) for n in blocks]
        pending, outs = [], [[] for _ in blocks]

        def consume():
            u, vb, g, s = pending.pop(0)
            sink = sink_ref[:, g * 2 * BLOCK:(g + 1) * 2 * BLOCK]
            if shifted:
                m = jnp.maximum(jnp.max(s, axis=0, keepdims=True), sink)
                e = jnp.exp2(s - m)
                den = jnp.sum(e, axis=0, keepdims=True) + jnp.exp2(sink - m)
            else:
                e = jnp.exp2(s)
                den = jnp.sum(e, axis=0, keepdims=True) + jnp.exp2(sink)
            on = _dot(vb, e.astype(BF16)) * (1.0 / den)
            yt = jnp.concatenate([on[:HEAD_DIM, :BLOCK], on[HEAD_DIM:, BLOCK:]], axis=0)
            outs[u].append(yt.T.astype(BF16))
            if g == SWA_GROUP - 1:
                y_ref[rows(blocks[u]), :] = jnp.concatenate(outs[u], axis=1)

        for u, (kb, vb, bias, qms) in enumerate(loaded):
            for g, qm in enumerate(qms):
                pending.append((u, vb, g, masked_scores(kb, qm, bias)))
                if len(pending) > SWA_LOOKAHEAD:
                    consume()
        while pending:
            consume()
        return carry

    lax.cond(bound_ref[pl.program_id(0)] < DIRECT_SCORE_LIMIT,
             lambda: lax.fori_loop(0, nq // unroll, functools.partial(step, shifted=False), 0),
             lambda: lax.fori_loop(0, nq // unroll, functools.partial(step, shifted=True), 0))


def _swa_attn(bound, qs, ks, vs, sink_vec, batch, seq):
    nq = seq // BLOCK
    return pl.pallas_call(
        functools.partial(_swa_kernel, seq=seq),
        grid=(batch,),
        in_specs=[pl.BlockSpec(memory_space=pltpu.SMEM),
                  pl.BlockSpec((1, nq, SQ, BLOCK), lambda b: (b, 0, 0, 0)),
                  pl.BlockSpec((seq, SK), lambda b: (b, 0)),
                  pl.BlockSpec((1, nq, SV, BLOCK), lambda b: (b, 0, 0, 0)),
                  pl.BlockSpec((1, SWA_Q_HEADS * BLOCK), lambda b: (0, 0))],
        out_specs=pl.BlockSpec((seq, SQ), lambda b: (b, 0)),
        out_shape=jax.ShapeDtypeStruct((batch * seq, SQ), BF16),
        compiler_params=pltpu.CompilerParams(
            dimension_semantics=("parallel",), vmem_limit_bytes=VMEM_LIMIT),
        name="swa_attn",
    )(bound, qs, ks, vs, sink_vec)


def _token_tile(n, want):
    return want if n % want == 0 else BLOCK


def kernel(x, positions, w_in, w_out, diff_lambda, diff_subln_g, swa_sink,
           ffn1_w_in, ffn1_w_out, ffn2_w_in, ffn2_w_out, ln_g, ln_b):
    batch, seq, _ = x.shape
    n = batch * seq
    tm = _token_tile(seq, 1024)
    tm_ffn = _token_tile(n, FFN_TOKEN_TILE)
    h = x.reshape(n, D_MODEL)
    cos, sin = _rope_tables(positions)

    w_in_p = _permute_in_proj(w_in).astype(BF16)
    w_out_p = _permute_out_proj(w_out).astype(BF16)
    ffn1_in, ffn1_out = ffn1_w_in.astype(BF16), ffn1_w_out.astype(BF16)
    ffn2_in, ffn2_out = ffn2_w_in.astype(BF16), ffn2_w_out.astype(BF16)
    sink = swa_sink.reshape(DEPTH, SWA_KV_HEADS, SWA_GROUP).transpose(0, 2, 1) * LOG2E
    sink_vec = jnp.repeat(sink.reshape(DEPTH, 1, SWA_Q_HEADS), BLOCK, axis=2)

    for l in range(DEPTH):
        lambda_init = 0.8 - 0.6 * math.exp(-0.3 * l)
        g = lambda i: ln_g[l, i].reshape(1, D_MODEL)
        b = lambda i: ln_b[l, i].reshape(1, D_MODEL)
        prm = jnp.zeros((SUBLANES, LANES), F32)
        prm = prm.at[0].set(diff_subln_g[l]).at[1].set(1.0 - lambda_init).at[2].set(lambda_init)

        h = _ffn_ln(h, ffn1_in, ffn1_out, g(0), b(0), l, tm_ffn)
        (qd, kd, vd, qs, ks, vs), diff_bound, swa_bound = _in_proj(
            h, w_in_p, cos, sin, l, batch, seq, tm)
        swa_bound = jnp.maximum(swa_bound, jnp.max(jnp.abs(sink_vec[l])))
        yd = _diff_attn(diff_bound, qd, kd, vd, diff_lambda[l], prm, batch, seq)
        ys = _swa_attn(swa_bound, qs, ks, vs, sink_vec[l], batch, seq)
        h = _mix_ffn_ln(h, yd, ys, w_out_p, g(1), b(1), ffn2_in, ffn2_out, g(2), b(2), l, tm_ffn)
    return h.reshape(batch, seq, D_MODEL)
```
